```python
import math
import jax, jax.numpy as jnp
from jax import lax
import numpy as np

D_MODEL = 1024
BATCH = 16
SEQ = 2048
DEPTH = 2

N_EVEN = (DEPTH + 1) // 2
N_ODD = DEPTH // 2

A_WIDTH = D_MODEL // 2
A_EXPAND = 128
A_HEADS = A_WIDTH // A_EXPAND
A_DK = A_EXPAND
A_DV = A_WIDTH // A_HEADS
A_CHUNK = 64
B_WIDTH = D_MODEL - A_WIDTH
B_GROUPS = 4
B_GROUP_DIM = B_WIDTH // B_GROUPS
B_CHUNK = 128
C_HEADS = 16
C_HEAD_DIM = D_MODEL // C_HEADS
C_ROT_DIM = C_HEAD_DIM // 4
ROPE_THETA = 500000.0
C_BRANCHES = ((128, 1), (512, 4), (2048, 16))
C_BLOCK = 128
D_FF = 4 * D_MODEL
EPS = 1e-6

EVEN_IN = 4 * A_WIDTH + 2 * B_WIDTH
ODD_IN = 3 * D_MODEL

kernel_name = "hybrid_hgrn2_gmlp_dilated_attn"

F32 = jnp.float32


def rmsnorm(x, g):
    xf = x.astype(F32)
    y = xf * lax.rsqrt(jnp.mean(xf * xf, axis=-1, keepdims=True) + EPS)
    return (y * g.astype(F32)).astype(x.dtype)


def layernorm(x, g, b):
    xf = x.astype(F32)
    mu = jnp.mean(xf, axis=-1, keepdims=True)
    var = jnp.mean(jnp.square(xf - mu), axis=-1, keepdims=True)
    return ((xf - mu) * lax.rsqrt(var + EPS) * g.astype(F32) + b.astype(F32)).astype(x.dtype)


def hgrn2_mix(q, f_logit, i, g, lb, norm_g):
    b_, s_, _ = q.shape
    n_chunks = s_ // A_CHUNK
    f = lb[None, None, :] + (1.0 - lb[None, None, :]) * jax.nn.sigmoid(f_logit.astype(F32))
    k = 1.0 - f
    logf = jnp.log(f)
    qf = jax.nn.silu(q.astype(F32))

    def chunks(t, d):
        return t.reshape(b_, n_chunks, A_CHUNK, A_HEADS, d).transpose(1, 0, 3, 2, 4)

    xs = (chunks(qf, A_DK), chunks(k, A_DK), chunks(i.astype(F32), A_DV), chunks(logf, A_DK))
    causal = np.tril(np.ones((A_CHUNK, A_CHUNK), dtype=bool))

    def step(state, inp):
        qc, kc, vc, lfc = inp
        G = jnp.cumsum(lfc, axis=2)
        o_inter = jnp.einsum('bhtk,bhkv->bhtv', qc * jnp.exp(G), state)
        diff = G[:, :, :, None, :] - G[:, :, None, :, :]
        decay = jnp.exp(jnp.where(causal[None, None, :, :, None], diff, -jnp.inf))
        scores = jnp.einsum('bhtk,bhsk,bhtsk->bhts', qc, kc, decay)
        o_intra = jnp.einsum('bhts,bhsv->bhtv', scores, vc)
        G_last = G[:, :, -1:, :]
        new_state = jnp.exp(G_last[:, :, 0, :])[..., None] * state + jnp.einsum(
            'bhsk,bhsv->bhkv', kc * jnp.exp(G_last - G), vc)
        return new_state, o_inter + o_intra

    state0 = jnp.zeros((b_, A_HEADS, A_DK, A_DV), F32)
    _, o = lax.scan(step, state0, xs)
    o = o.transpose(1, 0, 3, 2, 4).reshape(b_, s_, A_HEADS, A_DV)
    o = rmsnorm(o, norm_g.reshape(A_HEADS, A_DV))
    o = o.reshape(b_, s_, A_WIDTH) * jax.nn.silu(g.astype(F32))
    return o.astype(q.dtype)


def chunk_gmlp_mix(u, v, ln_g, ln_b, w_s, b_s):
    b_, s_, _ = u.shape
    v = layernorm(v, ln_g, ln_b)
    vb = v.reshape(b_, s_ // B_CHUNK, B_CHUNK, B_GROUPS, B_GROUP_DIM)
    tril = np.tril(np.ones((B_CHUNK, B_CHUNK), dtype=bool))
    w = jnp.where(tril[None], w_s, jnp.zeros_like(w_s))
    mixed = jnp.einsum('gts,bnsgc->bntgc', w, vb) + b_s.T[None, None, :, :, None]
    return u * mixed.reshape(b_, s_, B_WIDTH).astype(u.dtype)


def rope_partial(x, pos):
    half = C_ROT_DIM // 2
    inv = ROPE_THETA ** (-jnp.arange(half, dtype=F32) / half)
    ang = pos[..., None].astype(F32) * inv
    cos, sin = jnp.cos(ang)[:, :, None, :], jnp.sin(ang)[:, :, None, :]
    xf = x.astype(F32)
    x1, x2, xp = xf[..., :half], xf[..., half:C_ROT_DIM], xf[..., C_ROT_DIM:]
    return jnp.concatenate([x1 * cos - x2 * sin, x1 * sin + x2 * cos, xp], axis=-1)


def dilated_branch(q, k, v, window, dilation):
    b_, h_, s_, dh = q.shape
    L = s_ // dilation
    W = window // dilation
    qb_len = min(C_BLOCK, L)
    n_blk = L // qb_len

    def sub(t):
        return t.reshape(b_, h_, L, dilation, dh).transpose(0, 1, 3, 2, 4)

    qs = sub(q).reshape(b_, h_, dilation, n_blk, qb_len, dh)
    pad = ((0, 0), (0, 0), (0, 0), (W, 0), (0, 0))
    kp, vp = jnp.pad(sub(k), pad), jnp.pad(sub(v), pad)
    idx = np.arange(n_blk)[:, None] * qb_len + np.arange(qb_len + W)[None, :]
    kb = jnp.take(kp, idx, axis=3)
    vb = jnp.take(vp, idx, axis=3)
    q_pos = np.arange(n_blk)[:, None] * qb_len + np.arange(qb_len)[None, :]
    k_pos = idx - W
    dist = q_pos[:, :, None] - k_pos[:, None, :]
    mask = (dist >= 0) & (dist <= W) & (k_pos[:, None, :] >= 0)
    s = jnp.einsum('bhrnqd,bhrnkd->bhrnqk', qs, kb) * (1.0 / math.sqrt(dh))
    s = jnp.where(mask, s, -jnp.inf)
    m = jnp.max(s, axis=-1, keepdims=True)
    p = jnp.exp(s - m)
    den = jnp.sum(p, axis=-1, keepdims=True)
    o = jnp.einsum('bhrnqk,bhrnkd->bhrnqd', p, vb) / den

    def unsub(t):
        c = t.shape[-1]
        return t.reshape(b_, h_, dilation, L, c).transpose(0, 1, 3, 2, 4).reshape(b_, h_, s_, c)

    return unsub(o), unsub(m), unsub(den)


def dilated_attention_mix(h, w_in, pos):
    b_, s_, _ = h.shape
    qkv = h @ w_in
    q, k, v = jnp.split(qkv, 3, axis=-1)
    q = rope_partial(q.reshape(b_, s_, C_HEADS, C_HEAD_DIM), pos).transpose(0, 2, 1, 3)
    k = rope_partial(k.reshape(b_, s_, C_HEADS, C_HEAD_DIM), pos).transpose(0, 2, 1, 3)
    v = v.reshape(b_, s_, C_HEADS, C_HEAD_DIM).transpose(0, 2, 1, 3).astype(F32)
    outs, maxes, dens = [], [], []
    for window, dilation in C_BRANCHES:
        o_g, m_g, d_g = dilated_branch(q, k, v, window, dilation)
        outs.append(o_g); maxes.append(m_g); dens.append(d_g)
    m_all = jnp.max(jnp.stack(maxes, 0), axis=0)
    weights = [d_g * jnp.exp(m_g - m_all) for m_g, d_g in zip(maxes, dens)]
    o = sum(w_g * o_g for w_g, o_g in zip(weights, outs)) / sum(weights)
    return o.transpose(0, 2, 1, 3).reshape(b_, s_, D_MODEL).astype(h.dtype)


def setup_inputs(seed: int = 0) -> dict:
    key = jax.random.key(seed)
    ks = jax.random.split(key, 20)

    def nrm(k, shape, scale):
        return jax.random.normal(k, shape, F32) * scale

    def gain(k, shape):
        return 1.0 + 0.05 * jax.random.normal(k, shape, F32)

    return {
        'x': nrm(ks[0], (BATCH, SEQ, D_MODEL), 1.0),
        'positions': jnp.broadcast_to(jnp.arange(SEQ, dtype=jnp.int32), (BATCH, SEQ)),
        'norm_mix_pre': gain(ks[1], (DEPTH, D_MODEL)),
        'norm_mix_post': gain(ks[2], (DEPTH, D_MODEL)),
        'norm_ffn_pre': gain(ks[3], (DEPTH, D_MODEL)),
        'norm_ffn_post': gain(ks[4], (DEPTH, D_MODEL)),
        'w_in_even': nrm(ks[5], (N_EVEN, D_MODEL, EVEN_IN), D_MODEL ** -0.5),
        'lb_table': nrm(ks[6], (DEPTH + 1, A_WIDTH), 0.5),
        'a_norm': gain(ks[7], (N_EVEN, A_WIDTH)),
        'b_ln_g': gain(ks[8], (N_EVEN, B_WIDTH)),
        'b_ln_b': nrm(ks[9], (N_EVEN, B_WIDTH), 0.02),
        'b_ws': nrm(ks[10], (N_EVEN, B_GROUPS, B_CHUNK, B_CHUNK), B_CHUNK ** -0.5),
        'b_bias': 1.0 + nrm(ks[11], (N_EVEN, B_GROUPS, B_CHUNK), 0.1),
        'w_out_even': nrm(ks[12], (N_EVEN, A_WIDTH + B_WIDTH, D_MODEL), (A_WIDTH + B_WIDTH) ** -0.5),
        'w_in_odd': nrm(ks[13], (N_ODD, D_MODEL, ODD_IN), D_MODEL ** -0.5),
        'w_out_odd': nrm(ks[14], (N_ODD, D_MODEL, D_MODEL), D_MODEL ** -0.5),
        'w_ff1': nrm(ks[15], (DEPTH, D_MODEL, D_FF), D_MODEL ** -0.5),
        'w_ff2': nrm(ks[16], (DEPTH, D_FF, D_MODEL), D_FF ** -0.5),
    }


def reference(x, positions, norm_mix_pre, norm_mix_post, norm_ffn_pre, norm_ffn_post,
              w_in_even, lb_table, a_norm, b_ln_g, b_ln_b, b_ws, b_bias, w_out_even,
              w_in_odd, w_out_odd, w_ff1, w_ff2):
    lb_all = jnp.cumsum(jax.nn.softmax(lb_table.astype(F32), axis=0), axis=0)
    splits = [A_WIDTH, 2 * A_WIDTH, 3 * A_WIDTH, 4 * A_WIDTH, 4 * A_WIDTH + B_WIDTH]
    for l in range(DEPTH):
        h = rmsnorm(x, norm_mix_pre[l])
        if l % 2 == 0:
            e = l // 2
            proj = h @ w_in_even[e]
            qa, fa, ia, ga, ub, vb = jnp.split(proj, splits, axis=-1)
            oa = hgrn2_mix(qa, fa, ia, ga, lb_all[l], a_norm[e])
            ob = chunk_gmlp_mix(jax.nn.gelu(ub), jax.nn.gelu(vb), b_ln_g[e], b_ln_b[e], b_ws[e], b_bias[e])
            mix = jnp.concatenate([oa, ob], axis=-1) @ w_out_even[e]
        else:
            o = l // 2
            mix = dilated_attention_mix(h, w_in_odd[o], positions) @ w_out_odd[o]
        x = x + rmsnorm(mix, norm_mix_post[l])
        h = rmsnorm(x, norm_ffn_pre[l])
        y = jnp.square(jax.nn.relu(h @ w_ff1[l])) @ w_ff2[l]
        x = x + rmsnorm(y, norm_ffn_post[l])
    return x
```

```python
import functools

import numpy as np
import jax
import jax.numpy as jnp
from jax import lax
from jax.experimental import pallas as pl
from jax.experimental.pallas import tpu as pltpu

F32 = jnp.float32
BF16 = jnp.bfloat16

EPS = 1e-6
ROPE_THETA = 500000.0
NEG = -1e30

V7X_LANES = 128
V7X_VMEM_BYTES = 64 * 1024 * 1024
VMEM_LIMIT = V7X_VMEM_BYTES - 8 * 1024 * 1024

A_WIDTH = 512
A_HEADS = 4
A_DK = 128
B_WIDTH = 512
B_GROUPS = 4
CHUNK = 128
N_LEVELS = 8
C_HEAD_DIM = 64
ATT_BLOCK = 128
RESIDUES = 16
SLABS = 4
TOKEN_TILE = 512


def _dot(a, b):
    return jnp.dot(a, b, preferred_element_type=F32)


def _dot_nt(a, b):
    return lax.dot_general(a, b, (((1,), (1,)), ((), ())), preferred_element_type=F32)


def _dot_tn(a, b):
    return lax.dot_general(a, b, (((0,), (0,)), ((), ())), preferred_element_type=F32)


def _rms(x, g):
    ms = jnp.mean(x * x, axis=-1, keepdims=True)
    return x * lax.rsqrt(ms + EPS) * g


def _sigmoid(x):
    return 1.0 / (1.0 + jnp.exp(-x))


def _gelu_tanh(x):
    c = np.float32(np.sqrt(2.0 / np.pi))
    return x * (0.5 * (1.0 + jnp.tanh(c * (x + 0.044715 * (x * x * x)))))


def _resident(shape):
    nd = len(shape)
    return pl.BlockSpec(shape, lambda *_: (0,) * nd, pipeline_mode=pl.Buffered(1))


def _params(n_axes):
    return pltpu.CompilerParams(dimension_semantics=("arbitrary",) * n_axes,
                                vmem_limit_bytes=VMEM_LIMIT)


def _norm_proj_kernel(x_ref, g_ref, w_ref, o_ref):
    h = _rms(x_ref[...], g_ref[...]).astype(BF16)
    o_ref[...] = _dot(h, w_ref[...])


def _norm_proj(x, g, w):
    t, d = x.shape
    n = w.shape[1]
    return pl.pallas_call(
        _norm_proj_kernel,
        grid=(t // TOKEN_TILE,),
        in_specs=[pl.BlockSpec((TOKEN_TILE, d), lambda i: (i, 0)),
                  _resident((1, d)), _resident((d, n))],
        out_specs=pl.BlockSpec((TOKEN_TILE, n), lambda i: (i, 0)),
        out_shape=jax.ShapeDtypeStruct((t, n), F32),
        compiler_params=_params(1),
        name="norm_proj",
    )(x, g.reshape(1, d), w)


def _mixer_constants():
    c = CHUNK
    mats = [np.tril(np.ones((c, c), np.float32))]
    masks = [np.eye(c, dtype=np.float32)]
    idx = np.arange(c)
    for level in range(1, N_LEVELS):
        bs = 2 ** level
        half = bs // 2
        m = np.zeros((c, c), np.float32)
        for r in range(c):
            mid = (r // bs) * bs + half - 1
            if r % bs >= half:
                m[r, mid + 1:r + 1] = 1.0
            else:
                m[r, r + 1:mid + 1] = 1.0
        mats.append(m)
        same = (idx[:, None] // bs) == (idx[None, :] // bs)
        upper_t = (idx[:, None] % bs) >= half
        lower_s = (idx[None, :] % bs) < half
        masks.append((same & upper_t & lower_s).astype(np.float32))
    rev = np.zeros((c, c), np.float32)
    for r in range(c):
        rev[r, r + 1:] = 1.0
    mats.append(rev)
    return np.concatenate(mats, axis=0), np.stack(masks, axis=0)


def _mixer_kernel(proj_ref, lbt_ref, anorm_ref, lng_ref, lnb_ref, ws_ref, bb_ref, cm_ref, lm_ref,
                  o_ref, st_ref, r_ref):
    @pl.when(pl.program_id(1) == 0)
    def _():
        st_ref[...] = jnp.zeros_like(st_ref)

    t0, t1, t2 = lbt_ref[0:1, :], lbt_ref[1:2, :], lbt_ref[2:3, :]
    mx = jnp.maximum(jnp.maximum(t0, t1), t2)
    e0, e1, e2 = jnp.exp(t0 - mx), jnp.exp(t1 - mx), jnp.exp(t2 - mx)
    lb = e0 / (e0 + e1 + e2)

    w = A_WIDTH
    f = lb + (1.0 - lb) * _sigmoid(proj_ref[:, w:2 * w])
    logf = jnp.log(f)
    hi = logf.astype(BF16)
    lo = (logf - hi.astype(F32)).astype(BF16)
    cm = cm_ref[...]
    r_ref[...] = _dot(cm, hi) + _dot(cm, lo)

    c = CHUNK
    for h in range(A_HEADS):
        sl = slice(h * A_DK, (h + 1) * A_DK)
        q = proj_ref[:, sl]
        qf = q * _sigmoid(q)
        kk = 1.0 - f[:, sl]
        vb = proj_ref[:, 2 * w + h * A_DK:2 * w + (h + 1) * A_DK].astype(BF16)
        gate = proj_ref[:, 3 * w + h * A_DK:3 * w + (h + 1) * A_DK]

        a = lm_ref[0] * _dot_nt(qf.astype(BF16), kk.astype(BF16))
        for level in range(1, N_LEVELS):
            x = jnp.exp(r_ref[level * c:(level + 1) * c, sl])
            a = a + lm_ref[level] * _dot_nt((qf * x).astype(BF16), (kk * x).astype(BF16))

        eg = jnp.exp(r_ref[0:c, sl])
        st = st_ref[h]
        o = _dot_nt((qf * eg).astype(BF16), st.astype(BF16)) + _dot(a.astype(BF16), vb)
        khat = (kk * jnp.exp(r_ref[N_LEVELS * c:(N_LEVELS + 1) * c, sl])).astype(BF16)
        st_ref[h] = st * eg[c - 1:c, :] + _dot_tn(vb, khat)

        o = _rms(o, anorm_ref[:, sl])
        o_ref[:, sl] = (o * (gate * _sigmoid(gate))).astype(o_ref.dtype)

    gu = _gelu_tanh(proj_ref[:, 4 * w:4 * w + B_WIDTH])
    gv = _gelu_tanh(proj_ref[:, 4 * w + B_WIDTH:4 * w + 2 * B_WIDTH])
    mu = jnp.mean(gv, axis=-1, keepdims=True)
    d = gv - mu
    var = jnp.mean(d * d, axis=-1, keepdims=True)
    vn = (d * lax.rsqrt(var + EPS) * lng_ref[...] + lnb_ref[...]).astype(BF16)
    row = lax.broadcasted_iota(jnp.int32, (c, c), 0)
    col = lax.broadcasted_iota(jnp.int32, (c, c), 1)
    gd = B_WIDTH // B_GROUPS
    for g in range(B_GROUPS):
        sl = slice(g * gd, (g + 1) * gd)
        wg = jnp.where(row >= col, ws_ref[g], 0.0).astype(BF16)
        mixed = _dot(wg, vn[:, sl]) + bb_ref[:, g:g + 1]
        o_ref[:, A_WIDTH + g * gd:A_WIDTH + (g + 1) * gd] = (gu[:, sl] * mixed).astype(o_ref.dtype)


def _mixer(proj, batch, lb_table, a_norm, ln_g, ln_b, w_s, b_bias):
    t, n = proj.shape
    seq = t // batch
    n_chunks = seq // CHUNK
    cm, lm = _mixer_constants()
    width = A_WIDTH + B_WIDTH
    return pl.pallas_call(
        _mixer_kernel,
        grid=(batch, n_chunks),
        in_specs=[pl.BlockSpec((CHUNK, n), lambda b, c: (b * n_chunks + c, 0)),
                  _resident(lb_table.shape), _resident((1, A_WIDTH)),
                  _resident((1, B_WIDTH)), _resident((1, B_WIDTH)),
                  _resident(w_s.shape), _resident((CHUNK, B_GROUPS)),
                  _resident(cm.shape), _resident(lm.shape)],
        out_specs=pl.BlockSpec((CHUNK, width), lambda b, c: (b * n_chunks + c, 0)),
        out_shape=jax.ShapeDtypeStruct((t, width), BF16),
        scratch_shapes=[pltpu.VMEM((A_HEADS, A_DK, A_DK), F32),
                        pltpu.VMEM(((N_LEVELS + 1) * CHUNK, A_WIDTH), F32)],
        compiler_params=_params(2),
        name="mixer",
    )(proj, lb_table, a_norm.reshape(1, -1), ln_g.reshape(1, -1), ln_b.reshape(1, -1),
      w_s, b_bias.T, jnp.asarray(cm, BF16), jnp.asarray(lm, F32))


def _gather_slabs(ref, d):
    return jnp.concatenate([ref[:, s * d:(s + 1) * d] for s in range(SLABS)], axis=0)


def _out_ffn_kernel(mix_ref, x_ref, wo_ref, w1_ref, w2_ref, gpost_ref, gpre_ref, gffn_ref, o_ref, *, slabs):
    d = wo_ref.shape[1]
    x = _gather_slabs(x_ref, d) if slabs else x_ref[...]
    mix = _dot(mix_ref[...].astype(BF16), wo_ref[...])
    x1 = x + _rms(mix, gpost_ref[...])
    h = _rms(x1, gpre_ref[...]).astype(BF16)
    d_ff = w1_ref.shape[1]
    y = jnp.zeros_like(x1)
    for j in range(d_ff // d):
        hid = _dot(h, w1_ref[:, j * d:(j + 1) * d])
        hid = jnp.square(jnp.maximum(hid, 0.0)).astype(BF16)
        y = y + _dot(hid, w2_ref[j * d:(j + 1) * d, :])
    x2 = x1 + _rms(y, gffn_ref[...])
    if slabs:
        rows = x_ref.shape[0]
        for s in range(SLABS):
            o_ref[:, s * d:(s + 1) * d] = x2[s * rows:(s + 1) * rows, :]
    else:
        o_ref[...] = x2


def _out_ffn(mix, x, batch, w_out, w1, w2, g_post, g_pre, g_ffn, *, slabs):
    t, d = mix.shape
    d_ff = w1.shape[1]
    weights = [_resident((d, d)), _resident((d, d_ff)), _resident((d_ff, d)),
               _resident((1, d)), _resident((1, d)), _resident((1, d))]
    args = (w_out.astype(BF16), w1.astype(BF16), w2.astype(BF16),
            g_post.reshape(1, d), g_pre.reshape(1, d), g_ffn.reshape(1, d))
    kern = functools.partial(_out_ffn_kernel, slabs=slabs)
    if not slabs:
        return pl.pallas_call(
            kern,
            grid=(t // TOKEN_TILE,),
            in_specs=[pl.BlockSpec((TOKEN_TILE, d), lambda i: (i, 0)),
                      pl.BlockSpec((TOKEN_TILE, d), lambda i: (i, 0))] + weights,
            out_specs=pl.BlockSpec((TOKEN_TILE, d), lambda i: (i, 0)),
            out_shape=jax.ShapeDtypeStruct((t, d), F32),
            compiler_params=_params(1),
            name="out_ffn",
        )(mix, x, *args)
    per_seq = x.shape[1]
    n_tiles = RESIDUES // SLABS
    view = pl.BlockSpec((None, per_seq, SLABS * d), lambda b, r: (b, 0, r))
    return pl.pallas_call(
        kern,
        grid=(batch, n_tiles),
        in_specs=[pl.BlockSpec((SLABS * per_seq, d), lambda b, r: (b * n_tiles + r, 0)), view] + weights,
        out_specs=view,
        out_shape=jax.ShapeDtypeStruct(x.shape, F32),
        compiler_params=_params(2),
        name="out_ffn_residue_major",
    )(mix, x, *args)


def _qkv_rope_kernel(x_ref, g_ref, w_ref, cos_ref, sin_hi_ref, sin_lo_ref, o_ref):
    d = w_ref.shape[0]
    h = _rms(_gather_slabs(x_ref, d), g_ref[...]).astype(BF16)
    cos, sin_hi, sin_lo = cos_ref[...], sin_hi_ref[...], sin_lo_ref[...]
    lanes = V7X_LANES
    rot = C_HEAD_DIM // 8
    for j in range(3 * d // lanes):
        blk = _dot(h, w_ref[:, j * lanes:(j + 1) * lanes])
        if j < 2 * d // lanes:
            blk = (blk * cos + pltpu.roll(blk, rot, 1) * sin_hi
                   + pltpu.roll(blk, lanes - rot, 1) * sin_lo)
        if j < d // lanes:
            blk = blk * (1.0 / np.sqrt(C_HEAD_DIM))
        o_ref[:, j * lanes:(j + 1) * lanes] = blk


def _qkv_rope(x_view, g, w, cos, sin_hi, sin_lo):
    batch, per_seq, wide = x_view.shape
    d = wide // RESIDUES
    n = w.shape[1]
    n_tiles = RESIDUES // SLABS
    rows = SLABS * per_seq
    t = batch * RESIDUES * per_seq
    tab = pl.BlockSpec((rows, V7X_LANES), lambda b, r: (b * n_tiles + r, 0))
    return pl.pallas_call(
        _qkv_rope_kernel,
        grid=(batch, n_tiles),
        in_specs=[pl.BlockSpec((None, per_seq, SLABS * d), lambda b, r: (b, 0, r)),
                  _resident((1, d)), _resident((d, n)), tab, tab, tab],
        out_specs=pl.BlockSpec((rows, n), lambda b, r: (b * n_tiles + r, 0)),
        out_shape=jax.ShapeDtypeStruct((t, n), F32),
        compiler_params=_params(2),
        name="qkv_rope",
    )(x_view, g.reshape(1, d), w.astype(BF16), cos, sin_hi, sin_lo)


def _rope_tables(positions):
    batch, seq = positions.shape
    pos = positions.reshape(batch, seq // RESIDUES, RESIDUES).transpose(0, 2, 1).reshape(-1)
    half = C_HEAD_DIM // 8
    inv = ROPE_THETA ** (-jnp.arange(half, dtype=F32) / half)
    ang = pos[:, None].astype(F32) * inv
    cos, sin = jnp.cos(ang), jnp.sin(ang)
    one = jnp.ones((ang.shape[0], C_HEAD_DIM - 2 * half), F32)
    zero = jnp.zeros_like(cos)
    cos64 = jnp.concatenate([cos, cos, one], axis=1)
    hi64 = jnp.concatenate([zero, sin, 0.0 * one], axis=1)
    lo64 = jnp.concatenate([-sin, zero, 0.0 * one], axis=1)
    reps = V7X_LANES // C_HEAD_DIM
    return jnp.tile(cos64, (1, reps)), jnp.tile(hi64, (1, reps)), jnp.tile(lo64, (1, reps))


def _attention_biases():
    n = ATT_BLOCK
    out = {}
    for dil in (1, 4):
        chunks = RESIDUES // dil
        per = n // chunks
        rho = np.arange(n)
        pos = chunks * (rho % per) + rho // per
        kpos = np.concatenate([pos - n, pos])
        dist = pos[:, None] - kpos[None, :]
        out[dil] = np.where((dist >= 0) & (dist <= n), 0.0, NEG).astype(np.float32)
    rho = np.arange(n)
    out[16] = np.where(rho[:, None] >= rho[None, :], 0.0, NEG).astype(np.float32)
    first = np.concatenate([np.full((n, n), NEG, np.float32), np.zeros((n, n), np.float32)], axis=1)
    return out[16], out[4], out[1], first


def _attn_kernel(q_ref, k_ref, v_ref, b16_ref, b4_ref, b1_ref, first_ref, o_ref, m_s, l_s, acc_s):
    n = ATT_BLOCK
    lane = lax.broadcasted_iota(jnp.int32, (n, V7X_LANES), 1)
    head0 = lane < C_HEAD_DIM

    def block_attn(qblk, kblk, vblk, bias):
        kb, vb = kblk.astype(BF16), vblk.astype(BF16)
        res = []
        for h in range(2):
            keep = head0 if h == 0 else jnp.logical_not(head0)
            s = _dot_nt(jnp.where(keep, qblk, 0.0).astype(BF16), kb) + bias
            m = jnp.max(s, axis=-1, keepdims=True)
            p = jnp.exp(s - m)
            l = jnp.sum(p, axis=-1, keepdims=True)
            res.append((m, l, _dot(p.astype(BF16), vb)))
        return tuple(jnp.where(head0, a, b) for a, b in zip(res[0], res[1]))

    def gather(ref, starts, size):
        return jnp.concatenate([ref[pl.ds(s, size), :] for s in starts], axis=0)

    def merge(old, new):
        m_o, l_o, a_o = old
        m_n, l_n, a_n = new
        m = jnp.maximum(m_o, m_n)
        w_o, w_n = jnp.exp(m_o - m), jnp.exp(m_n - m)
        return m, l_o * w_o + l_n * w_n, a_o * w_o + a_n * w_n

    def body16(r, carry):
        st = pl.multiple_of(r * n, n)
        m, l, acc = block_attn(q_ref[pl.ds(st, n), :], k_ref[pl.ds(st, n), :], v_ref[pl.ds(st, n), :],
                               b16_ref[...])
        m_s[pl.ds(st, n), :] = m
        l_s[pl.ds(st, n), :] = l
        acc_s[pl.ds(st, n), :] = acc
        return carry

    lax.fori_loop(0, RESIDUES, body16, 0)

    def strided_block(dil, idx, bias_ref):
        chunks = RESIDUES // dil
        per = n // chunks
        blocks_per_res = RESIDUES // dil
        res = lax.div(idx, blocks_per_res)
        blk = lax.rem(idx, blocks_per_res)
        prev = jnp.maximum(blk - 1, 0)
        starts = [pl.multiple_of((res + dil * c) * n + per * blk, per) for c in range(chunks)]
        pstarts = [pl.multiple_of((res + dil * c) * n + per * prev, per) for c in range(chunks)]
        kcat = jnp.concatenate([gather(k_ref, pstarts, per), gather(k_ref, starts, per)], axis=0)
        vcat = jnp.concatenate([gather(v_ref, pstarts, per), gather(v_ref, starts, per)], axis=0)
        bias = bias_ref[...] + jnp.where(blk == 0, first_ref[...], 0.0)
        new = block_attn(gather(q_ref, starts, per), kcat, vcat, bias)
        old = (gather(m_s, starts, per), gather(l_s, starts, per), gather(acc_s, starts, per))
        return starts, per, merge(old, new)

    def body4(idx, carry):
        starts, per, (m, l, acc) = strided_block(4, idx, b4_ref)
        for c, s in enumerate(starts):
            m_s[pl.ds(s, per), :] = m[c * per:(c + 1) * per, :]
            l_s[pl.ds(s, per), :] = l[c * per:(c + 1) * per, :]
            acc_s[pl.ds(s, per), :] = acc[c * per:(c + 1) * per, :]
        return carry

    lax.fori_loop(0, RESIDUES, body4, 0)

    def body1(idx, carry):
        starts, per, (m, l, acc) = strided_block(1, idx, b1_ref)
        out = acc / l
        for c, s in enumerate(starts):
            o_ref[pl.ds(s, per), :] = out[c * per:(c + 1) * per, :]
        return carry

    lax.fori_loop(0, RESIDUES, body1, 0)


def _attention(qkv, batch):
    t, n3 = qkv.shape
    seq = t // batch
    d = n3 // 3
    pairs = d // V7X_LANES
    qkv3 = qkv.reshape(batch, seq, n3)
    consts = [jnp.asarray(c) for c in _attention_biases()]

    def col(off):
        return pl.BlockSpec((None, seq, V7X_LANES), lambda b, p: (b, 0, off + p))

    out = pl.pallas_call(
        _attn_kernel,
        grid=(batch, pairs),
        in_specs=[col(0), col(pairs), col(2 * pairs)] + [_resident(c.shape) for c in consts],
        out_specs=pl.BlockSpec((None, seq, V7X_LANES), lambda b, p: (b, 0, p)),
        out_shape=jax.ShapeDtypeStruct((batch, seq, d), F32),
        scratch_shapes=[pltpu.VMEM((seq, V7X_LANES), F32)] * 3,
        compiler_params=_params(2),
        name="dilated_attention",
    )(qkv3, qkv3, qkv3, *consts)
    return out.reshape(t, d)


def kernel(x, positions, norm_mix_pre, norm_mix_post, norm_ffn_pre, norm_ffn_post, w_in_even, lb_table,
           a_norm, b_ln_g, b_ln_b, b_ws, b_bias, w_out_even, w_in_odd, w_out_odd, w_ff1, w_ff2):
    batch, seq, d = x.shape
    assert seq == RESIDUES * ATT_BLOCK and d == A_WIDTH + B_WIDTH
    assert norm_mix_pre.shape[0] == 2 and lb_table.shape[0] == 3
    t = batch * seq
    xf = x.reshape(t, d)

    proj = _norm_proj(xf, norm_mix_pre[0], w_in_even[0].astype(BF16))
    mix = _mixer(proj, batch, lb_table, a_norm[0], b_ln_g[0], b_ln_b[0], b_ws[0], b_bias[0])
    x1 = _out_ffn(mix, xf, batch, w_out_even[0], w_ff1[0], w_ff2[0],
                  norm_mix_post[0], norm_ffn_pre[0], norm_ffn_post[0], slabs=False)

    x_view = x1.reshape(batch, seq // RESIDUES, RESIDUES * d)
    cos, sin_hi, sin_lo = _rope_tables(positions)
    qkv = _qkv_rope(x_view, norm_mix_pre[1], w_in_odd[0], cos, sin_hi, sin_lo)
    att = _attention(qkv, batch)
    out = _out_ffn(att, x_view, batch, w_out_odd[0], w_ff1[1], w_ff2[1],
                   norm_mix_post[1], norm_ffn_pre[1], norm_ffn_post[1], slabs=True)
    return out.reshape(batch, seq, d)
```

```python
import functools

import numpy as np
import jax
import jax.numpy as jnp
from jax import lax
from jax.experimental import pallas as pl
from jax.experimental.pallas import tpu as pltpu

F32 = jnp.float32
BF16 = jnp.bfloat16

EPS = 1e-6
ROPE_THETA = 500000.0
NEG = -1e30

V7X_LANES = 128
V7X_MXU_WIDTH = 256
V7X_VMEM_BYTES = 64 * 1024 * 1024
VMEM_LIMIT = V7X_VMEM_BYTES - 8 * 1024 * 1024

A_WIDTH = 512
A_HEADS = 4
A_DK = 128
B_WIDTH = 512
B_GROUPS = 4
CHUNK = 128
N_LEVELS = 8
C_HEAD_DIM = 64
ATT_BLOCK = 128
ATT_UNROLL = 4
RESIDUES = 16
TOKEN_TILE = 512
TILE_PER_RESIDUE = TOKEN_TILE // RESIDUES


def _dot(a, b):
    return jnp.dot(a, b, preferred_element_type=F32)


def _dot_nt(a, b):
    return lax.dot_general(a, b, (((1,), (1,)), ((), ())), preferred_element_type=F32)


def _dot_tn(a, b):
    return lax.dot_general(a, b, (((0,), (0,)), ((), ())), preferred_element_type=F32)


def _rms(x, g):
    ms = jnp.mean(x * x, axis=-1, keepdims=True)
    return x * lax.rsqrt(ms + EPS) * g


def _sigmoid(x):
    return 1.0 / (1.0 + jnp.exp(-x))


def _gelu_tanh(x):
    c = np.float32(np.sqrt(2.0 / np.pi))
    return x * (0.5 * (1.0 + jnp.tanh(c * (x + 0.044715 * (x * x * x)))))


def _resident(shape):
    nd = len(shape)
    return pl.BlockSpec(shape, lambda *_: (0,) * nd, pipeline_mode=pl.Buffered(1))


def _params(n_axes):
    return pltpu.CompilerParams(dimension_semantics=("arbitrary",) * n_axes,
                                vmem_limit_bytes=VMEM_LIMIT)


def _norm_proj_kernel(x_ref, g_ref, w_ref, o_ref):
    h = _rms(x_ref[...], g_ref[...]).astype(BF16)
    o_ref[...] = _dot(h, w_ref[...])


def _norm_proj(x, g, w):
    t, d = x.shape
    n = w.shape[1]
    return pl.pallas_call(
        _norm_proj_kernel,
        grid=(t // TOKEN_TILE,),
        in_specs=[pl.BlockSpec((TOKEN_TILE, d), lambda i: (i, 0)),
                  _resident((1, d)), _resident((d, n))],
        out_specs=pl.BlockSpec((TOKEN_TILE, n), lambda i: (i, 0)),
        out_shape=jax.ShapeDtypeStruct((t, n), F32),
        compiler_params=_params(1),
        name="norm_proj",
    )(x, g.reshape(1, d), w)


def _mixer_constants():
    c = CHUNK
    mats = [np.tril(np.ones((c, c), np.float32))]
    masks = [np.eye(c, dtype=np.float32)]
    idx = np.arange(c)
    for level in range(1, N_LEVELS):
        bs = 2 ** level
        half = bs // 2
        m = np.zeros((c, c), np.float32)
        for r in range(c):
            mid = (r // bs) * bs + half - 1
            if r % bs >= half:
                m[r, mid + 1:r + 1] = 1.0
            else:
                m[r, r + 1:mid + 1] = 1.0
        mats.append(m)
        same = (idx[:, None] // bs) == (idx[None, :] // bs)
        upper_t = (idx[:, None] % bs) >= half
        lower_s = (idx[None, :] % bs) < half
        masks.append((same & upper_t & lower_s).astype(np.float32))
    rev = np.zeros((c, c), np.float32)
    for r in range(c):
        rev[r, r + 1:] = 1.0
    mats.append(rev)
    return np.concatenate(mats, axis=0), np.stack(masks, axis=0)


def _mixer_kernel(proj_ref, lbt_ref, anorm_ref, lng_ref, lnb_ref, ws_ref, bb_ref, cm_ref, lm_ref,
                  o_ref, st_ref, r_ref):
    @pl.when(pl.program_id(1) == 0)
    def _():
        st_ref[...] = jnp.zeros_like(st_ref)

    t0, t1, t2 = lbt_ref[0:1, :], lbt_ref[1:2, :], lbt_ref[2:3, :]
    mx = jnp.maximum(jnp.maximum(t0, t1), t2)
    e0, e1, e2 = jnp.exp(t0 - mx), jnp.exp(t1 - mx), jnp.exp(t2 - mx)
    lb = e0 / (e0 + e1 + e2)

    w = A_WIDTH
    f = lb + (1.0 - lb) * _sigmoid(proj_ref[:, w:2 * w])
    logf = jnp.log(f)
    hi = logf.astype(BF16)
    lo = (logf - hi.astype(F32)).astype(BF16)
    cm = cm_ref[...]
    r_ref[...] = _dot(cm, hi) + _dot(cm, lo)

    c = CHUNK
    for h in range(A_HEADS):
        sl = slice(h * A_DK, (h + 1) * A_DK)
        q = proj_ref[:, sl]
        qf = q * _sigmoid(q)
        kk = 1.0 - f[:, sl]
        vb = proj_ref[:, 2 * w + h * A_DK:2 * w + (h + 1) * A_DK].astype(BF16)
        gate = proj_ref[:, 3 * w + h * A_DK:3 * w + (h + 1) * A_DK]

        a = lm_ref[0] * _dot_nt(qf.astype(BF16), kk.astype(BF16))
        for level in range(1, N_LEVELS):
            x = jnp.exp(r_ref[level * c:(level + 1) * c, sl])
            a = a + lm_ref[level] * _dot_nt((qf * x).astype(BF16), (kk * x).astype(BF16))

        eg = jnp.exp(r_ref[0:c, sl])
        st = st_ref[h]
        o = _dot_nt((qf * eg).astype(BF16), st.astype(BF16)) + _dot(a.astype(BF16), vb)
        khat = (kk * jnp.exp(r_ref[N_LEVELS * c:(N_LEVELS + 1) * c, sl])).astype(BF16)
        st_ref[h] = st * eg[c - 1:c, :] + _dot_tn(vb, khat)

        o = _rms(o, anorm_ref[:, sl])
        o_ref[:, sl] = (o * (gate * _sigmoid(gate))).astype(o_ref.dtype)

    gu = _gelu_tanh(proj_ref[:, 4 * w:4 * w + B_WIDTH])
    gv = _gelu_tanh(proj_ref[:, 4 * w + B_WIDTH:4 * w + 2 * B_WIDTH])
    mu = jnp.mean(gv, axis=-1, keepdims=True)
    d = gv - mu
    var = jnp.mean(d * d, axis=-1, keepdims=True)
    vn = (d * lax.rsqrt(var + EPS) * lng_ref[...] + lnb_ref[...]).astype(BF16)
    row = lax.broadcasted_iota(jnp.int32, (c, c), 0)
    col = lax.broadcasted_iota(jnp.int32, (c, c), 1)
    gd = B_WIDTH // B_GROUPS
    for g in range(B_GROUPS):
        sl = slice(g * gd, (g + 1) * gd)
        wg = jnp.where(row >= col, ws_ref[g], 0.0).astype(BF16)
        mixed = _dot(wg, vn[:, sl]) + bb_ref[:, g:g + 1]
        o_ref[:, A_WIDTH + g * gd:A_WIDTH + (g + 1) * gd] = (gu[:, sl] * mixed).astype(o_ref.dtype)


def _mixer(proj, batch, lb_table, a_norm, ln_g, ln_b, w_s, b_bias):
    t, n = proj.shape
    seq = t // batch
    n_chunks = seq // CHUNK
    cm, lm = _mixer_constants()
    width = A_WIDTH + B_WIDTH
    return pl.pallas_call(
        _mixer_kernel,
        grid=(batch, n_chunks),
        in_specs=[pl.BlockSpec((CHUNK, n), lambda b, c: (b * n_chunks + c, 0)),
                  _resident(lb_table.shape), _resident((1, A_WIDTH)),
                  _resident((1, B_WIDTH)), _resident((1, B_WIDTH)),
                  _resident(w_s.shape), _resident((CHUNK, B_GROUPS)),
                  _resident(cm.shape), _resident(lm.shape)],
        out_specs=pl.BlockSpec((CHUNK, width), lambda b, c: (b * n_chunks + c, 0)),
        out_shape=jax.ShapeDtypeStruct((t, width), BF16),
        scratch_shapes=[pltpu.VMEM((A_HEADS, A_DK, A_DK), F32),
                        pltpu.VMEM(((N_LEVELS + 1) * CHUNK, A_WIDTH), F32)],
        compiler_params=_params(2),
        name="mixer",
    )(proj, lb_table, a_norm.reshape(1, -1), ln_g.reshape(1, -1), ln_b.reshape(1, -1),
      w_s, b_bias.T, jnp.asarray(cm, BF16), jnp.asarray(lm, F32))


def _out_ffn_kernel(mix_ref, x_ref, wo_ref, w1_ref, w2_ref, gpost_ref, gpre_ref, gffn_ref, o_ref, slab_ref,
                    *, to_residue_major):
    d = wo_ref.shape[1]
    per = TILE_PER_RESIDUE
    if to_residue_major:
        x, mix = x_ref[...], mix_ref[...]
    else:
        x = jnp.concatenate([x_ref[r] for r in range(RESIDUES)], axis=0)
        mix = jnp.concatenate([mix_ref[r] for r in range(RESIDUES)], axis=0)
    x1 = x + _rms(_dot(mix.astype(BF16), wo_ref[...]), gpost_ref[...])
    h = _rms(x1, gpre_ref[...]).astype(BF16)
    d_ff = w1_ref.shape[1]
    y = jnp.zeros_like(x1)
    for j in range(d_ff // d):
        hid = _dot(h, w1_ref[:, j * d:(j + 1) * d])
        hid = jnp.square(jnp.maximum(hid, 0.0)).astype(BF16)
        y = y + _dot(hid, w2_ref[j * d:(j + 1) * d, :])
    x2 = x1 + _rms(y, gffn_ref[...])
    lanes = V7X_LANES
    for cb in range(d // lanes):
        cols = slice(cb * lanes, (cb + 1) * lanes)
        if to_residue_major:
            slab_ref[cb] = x2[:, cols]
            for r in range(RESIDUES):
                o_ref[r, :, cols] = slab_ref[cb, pl.ds(r, per, stride=RESIDUES), :]
        else:
            for r in range(RESIDUES):
                slab_ref[cb, pl.ds(r, per, stride=RESIDUES), :] = x2[r * per:(r + 1) * per, cols]
            o_ref[:, cols] = slab_ref[cb]


def _out_ffn(mix, x, w_out, w1, w2, g_post, g_pre, g_ffn, *, to_residue_major):
    d = w_out.shape[1]
    d_ff = w1.shape[1]
    per = TILE_PER_RESIDUE
    weights = [_resident((d, d)), _resident((d, d_ff)), _resident((d_ff, d)),
               _resident((1, d)), _resident((1, d)), _resident((1, d))]
    args = (w_out.astype(BF16), w1.astype(BF16), w2.astype(BF16),
            g_post.reshape(1, d), g_pre.reshape(1, d), g_ffn.reshape(1, d))
    scratch = [pltpu.VMEM((d // V7X_LANES, TOKEN_TILE, V7X_LANES), F32)]
    kern = functools.partial(_out_ffn_kernel, to_residue_major=to_residue_major)
    if to_residue_major:
        batch, seq, _ = x.shape
        tiles = seq // TOKEN_TILE
        nat = pl.BlockSpec((None, TOKEN_TILE, d), lambda b, j: (b, j, 0))
        return pl.pallas_call(
            kern,
            grid=(batch, tiles),
            in_specs=[nat, nat] + weights,
            out_specs=pl.BlockSpec((None, RESIDUES, per, d), lambda b, j: (b, 0, j, 0)),
            out_shape=jax.ShapeDtypeStruct((batch, RESIDUES, seq // RESIDUES, d), F32),
            scratch_shapes=scratch,
            compiler_params=_params(2),
            name="out_ffn_to_residue_major",
        )(mix, x, *args)
    batch, _, per_seq, _ = x.shape
    tiles = per_seq // per
    res = pl.BlockSpec((None, RESIDUES, per, d), lambda b, j: (b, 0, j, 0))
    return pl.pallas_call(
        kern,
        grid=(batch, tiles),
        in_specs=[res, res] + weights,
        out_specs=pl.BlockSpec((None, TOKEN_TILE, d), lambda b, j: (b, j, 0)),
        out_shape=jax.ShapeDtypeStruct((batch, RESIDUES * per_seq, d), F32),
        scratch_shapes=scratch,
        compiler_params=_params(2),
        name="out_ffn_from_residue_major",
    )(mix, x, *args)


def _qkv_rope_kernel(x_ref, g_ref, w_ref, cos_ref, sin_hi_ref, sin_lo_ref, o_ref):
    d = w_ref.shape[0]
    h = _rms(x_ref[...], g_ref[...]).astype(BF16)
    cos, sin_hi, sin_lo = cos_ref[...], sin_hi_ref[...], sin_lo_ref[...]
    lanes = V7X_LANES
    rot = C_HEAD_DIM // 8
    wide = V7X_MXU_WIDTH
    for j in range(3 * d // wide):
        blk = _dot(h, w_ref[:, j * wide:(j + 1) * wide])
        for half in range(wide // lanes):
            col = j * wide + half * lanes
            part = blk[:, half * lanes:(half + 1) * lanes]
            if col < 2 * d:
                part = (part * cos + pltpu.roll(part, rot, 1) * sin_hi
                        + pltpu.roll(part, lanes - rot, 1) * sin_lo)
            if col < d:
                part = part * (1.0 / np.sqrt(C_HEAD_DIM))
            o_ref[:, col:col + lanes] = part


def _qkv_rope(x, g, w, cos, sin_hi, sin_lo):
    t, d = x.shape
    n = w.shape[1]
    tab = pl.BlockSpec((TOKEN_TILE, V7X_LANES), lambda i: (i, 0))
    return pl.pallas_call(
        _qkv_rope_kernel,
        grid=(t // TOKEN_TILE,),
        in_specs=[pl.BlockSpec((TOKEN_TILE, d), lambda i: (i, 0)),
                  _resident((1, d)), _resident((d, n)), tab, tab, tab],
        out_specs=pl.BlockSpec((TOKEN_TILE, n), lambda i: (i, 0)),
        out_shape=jax.ShapeDtypeStruct((t, n), F32),
        compiler_params=_params(1),
        name="qkv_rope",
    )(x, g.reshape(1, d), w.astype(BF16), cos, sin_hi, sin_lo)


def _rope_tables(positions):
    batch, seq = positions.shape
    pos = positions.reshape(batch, seq // RESIDUES, RESIDUES).transpose(0, 2, 1).reshape(-1)
    half = C_HEAD_DIM // 8
    inv = ROPE_THETA ** (-jnp.arange(half, dtype=F32) / half)
    ang = pos[:, None].astype(F32) * inv
    cos, sin = jnp.cos(ang), jnp.sin(ang)
    one = jnp.ones((ang.shape[0], C_HEAD_DIM - 2 * half), F32)
    zero = jnp.zeros_like(cos)
    cos64 = jnp.concatenate([cos, cos, one], axis=1)
    hi64 = jnp.concatenate([zero, sin, 0.0 * one], axis=1)
    lo64 = jnp.concatenate([-sin, zero, 0.0 * one], axis=1)
    reps = V7X_LANES // C_HEAD_DIM
    return jnp.tile(cos64, (1, reps)), jnp.tile(hi64, (1, reps)), jnp.tile(lo64, (1, reps))


def _attention_biases():
    n = ATT_BLOCK
    out = {}
    for dil in (1, 4):
        chunks = RESIDUES // dil
        per = n // chunks
        rho = np.arange(n)
        pos = chunks * (rho % per) + rho // per
        kpos = np.concatenate([pos - n, pos])
        dist = pos[:, None] - kpos[None, :]
        out[dil] = np.where((dist >= 0) & (dist <= n), 0.0, NEG).astype(np.float32)
    rho = np.arange(n)
    out[16] = np.where(rho[:, None] >= rho[None, :], 0.0, NEG).astype(np.float32)
    first = np.concatenate([np.full((n, n), NEG, np.float32), np.zeros((n, n), np.float32)], axis=1)
    return tuple(np.concatenate([b, b], axis=0) for b in (out[16], out[4], out[1], first))


def _attn_kernel(q_ref, k_ref, v_ref, b16_ref, b4_ref, b1_ref, first_ref, o_ref, m_s, l_s, acc_s):
    n = ATT_BLOCK
    lanes = V7X_LANES
    head0 = lax.broadcasted_iota(jnp.int32, (n, lanes), 1) < C_HEAD_DIM

    def attend(blocks):
        scores = []
        for q, k, _, bias in blocks:
            q2 = jnp.concatenate([jnp.where(head0, q, 0.0), jnp.where(head0, 0.0, q)], axis=0)
            scores.append(_dot_nt(q2.astype(BF16), k.astype(BF16)) + bias)
        outs = []
        for (_, _, v, _), s in zip(blocks, scores):
            m = jnp.max(s, axis=-1, keepdims=True)
            p = jnp.exp(s - m).astype(BF16)
            va = jnp.concatenate([v.astype(BF16), jnp.ones(v.shape, BF16)], axis=1)
            pv = _dot(p, va)
            outs.append((jnp.where(head0, m[:n], m[n:]),
                         jnp.where(head0, pv[:n, lanes:], pv[n:, lanes:]),
                         jnp.where(head0, pv[:n, :lanes], pv[n:, :lanes])))
        return outs

    def gather(ref, starts, size):
        return jnp.concatenate([ref[pl.ds(s, size), :] for s in starts], axis=0)

    def merge(old, new):
        m_o, l_o, a_o = old
        m_n, l_n, a_n = new
        m = jnp.maximum(m_o, m_n)
        w_o, w_n = jnp.exp(m_o - m), jnp.exp(m_n - m)
        return m, l_o * w_o + l_n * w_n, a_o * w_o + a_n * w_n

    def body16(i, carry):
        starts = [pl.multiple_of((i * ATT_UNROLL + u) * n, n) for u in range(ATT_UNROLL)]
        blocks = [(q_ref[pl.ds(s, n), :], k_ref[pl.ds(s, n), :], v_ref[pl.ds(s, n), :], b16_ref[...])
                  for s in starts]
        for s, (m, l, acc) in zip(starts, attend(blocks)):
            m_s[pl.ds(s, n), :] = m
            l_s[pl.ds(s, n), :] = l
            acc_s[pl.ds(s, n), :] = acc
        return carry

    lax.fori_loop(0, RESIDUES // ATT_UNROLL, body16, 0)

    def strided_blocks(dil, res, first_blk, bias_ref):
        chunks = RESIDUES // dil
        per = n // chunks

        def starts_of(blk):
            return [pl.multiple_of((res + dil * c) * n + per * blk, per) for c in range(chunks)]

        static_first = isinstance(first_blk, int)
        prev0 = max(first_blk - 1, 0) if static_first else jnp.maximum(first_blk - 1, 0)
        k_prev, v_prev = gather(k_ref, starts_of(prev0), per), gather(v_ref, starts_of(prev0), per)
        blocks, all_starts = [], []
        for u in range(ATT_UNROLL):
            starts = starts_of(first_blk + u)
            k_cur, v_cur = gather(k_ref, starts, per), gather(v_ref, starts, per)
            bias = bias_ref[...]
            if u == 0:
                if static_first:
                    bias = bias + first_ref[...] if first_blk == 0 else bias
                else:
                    bias = bias + jnp.where(first_blk == 0, first_ref[...], 0.0)
            blocks.append((gather(q_ref, starts, per), jnp.concatenate([k_prev, k_cur], axis=0),
                           jnp.concatenate([v_prev, v_cur], axis=0), bias))
            all_starts.append(starts)
            k_prev, v_prev = k_cur, v_cur
        merged = []
        for starts, new in zip(all_starts, attend(blocks)):
            old = (gather(m_s, starts, per), gather(l_s, starts, per), gather(acc_s, starts, per))
            merged.append((starts, per, merge(old, new)))
        return merged

    def body4(res, carry):
        for starts, per, (m, l, acc) in strided_blocks(4, res, 0, b4_ref):
            for c, s in enumerate(starts):
                m_s[pl.ds(s, per), :] = m[c * per:(c + 1) * per, :]
                l_s[pl.ds(s, per), :] = l[c * per:(c + 1) * per, :]
                acc_s[pl.ds(s, per), :] = acc[c * per:(c + 1) * per, :]
        return carry

    lax.fori_loop(0, 4, body4, 0)

    def body1(i, carry):
        for starts, per, (m, l, acc) in strided_blocks(1, 0, i * ATT_UNROLL, b1_ref):
            out = acc / l
            for c, s in enumerate(starts):
                o_ref[pl.ds(s, per), :] = out[c * per:(c + 1) * per, :]
        return carry

    lax.fori_loop(0, RESIDUES // ATT_UNROLL, body1, 0)


def _attention(qkv, batch):
    t, n3 = qkv.shape
    seq = t // batch
    d = n3 // 3
    pairs = d // V7X_LANES
    qkv3 = qkv.reshape(batch, seq, n3)
    consts = [jnp.asarray(c) for c in _attention_biases()]

    def col(off):
        return pl.BlockSpec((None, seq, V7X_LANES), lambda b, p: (b, 0, off + p))

    return pl.pallas_call(
        _attn_kernel,
        grid=(batch, pairs),
        in_specs=[col(0), col(pairs), col(2 * pairs)] + [_resident(c.shape) for c in consts],
        out_specs=pl.BlockSpec((None, seq, V7X_LANES), lambda b, p: (b, 0, p)),
        out_shape=jax.ShapeDtypeStruct((batch, seq, d), F32),
        scratch_shapes=[pltpu.VMEM((seq, V7X_LANES), F32)] * 3,
        compiler_params=_params(2),
        name="dilated_attention",
    )(qkv3, qkv3, qkv3, *consts)


def kernel(x, positions, norm_mix_pre, norm_mix_post, norm_ffn_pre, norm_ffn_post, w_in_even, lb_table,
           a_norm, b_ln_g, b_ln_b, b_ws, b_bias, w_out_even, w_in_odd, w_out_odd, w_ff1, w_ff2):
    batch, seq, d = x.shape
    assert seq == RESIDUES * ATT_BLOCK and d == A_WIDTH + B_WIDTH
    assert norm_mix_pre.shape[0] == 2 and lb_table.shape[0] == 3
    t = batch * seq

    proj = _norm_proj(x.reshape(t, d), norm_mix_pre[0], w_in_even[0].astype(BF16))
    mix = _mixer(proj, batch, lb_table, a_norm[0], b_ln_g[0], b_ln_b[0], b_ws[0], b_bias[0])
    x1 = _out_ffn(mix.reshape(batch, seq, d), x, w_out_even[0], w_ff1[0], w_ff2[0],
                  norm_mix_post[0], norm_ffn_pre[0], norm_ffn_post[0], to_residue_major=True)

    cos, sin_hi, sin_lo = _rope_tables(positions)
    qkv = _qkv_rope(x1.reshape(t, d), norm_mix_pre[1], w_in_odd[0], cos, sin_hi, sin_lo)
    att = _attention(qkv, batch).reshape(x1.shape)
    return _out_ffn(att, x1, w_out_odd[0], w_ff1[1], w_ff2[1],
                    norm_mix_post[1], norm_ffn_pre[1], norm_ffn_post[1], to_residue_major=False)
```

```python
import functools

import numpy as np
import jax
import jax.numpy as jnp
from jax import lax
from jax.experimental import pallas as pl
from jax.experimental.pallas import tpu as pltpu

F32 = jnp.float32
BF16 = jnp.bfloat16

EPS = 1e-6
ROPE_THETA = 500000.0
NEG = -1e30
LOG2E = float(np.log2(np.e))

V7X_LANES = 128
V7X_MXU_WIDTH = 256
V7X_VMEM_BYTES = 64 * 1024 * 1024
VMEM_LIMIT = V7X_VMEM_BYTES - 8 * 1024 * 1024

A_WIDTH = 512
A_HEADS = 4
A_DK = 128
B_WIDTH = 512
B_GROUPS = 4
CHUNK = 128
N_LEVELS = 8
C_HEAD_DIM = 64
ATT_BLOCK = 128
ATT_UNROLL = 8
RESIDUES = 16
TOKEN_TILE = 512
TILE_PER_RESIDUE = TOKEN_TILE // RESIDUES
FFN_PARTS = 2


def _dot(a, b):
    return jnp.dot(a, b, preferred_element_type=F32)


def _dot_nt(a, b):
    return lax.dot_general(a, b, (((1,), (1,)), ((), ())), preferred_element_type=F32)


def _dot_tn(a, b):
    return lax.dot_general(a, b, (((0,), (0,)), ((), ())), preferred_element_type=F32)


def _rms(x, g):
    ms = jnp.mean(x * x, axis=-1, keepdims=True)
    return x * lax.rsqrt(ms + EPS) * g


def _sigmoid(x):
    return 1.0 / (1.0 + jnp.exp(-x))


def _gelu_tanh(x):
    c = np.float32(np.sqrt(2.0 / np.pi))
    return x * (0.5 * (1.0 + jnp.tanh(c * (x + 0.044715 * (x * x * x)))))


def _resident(shape):
    nd = len(shape)
    return pl.BlockSpec(shape, lambda *_: (0,) * nd, pipeline_mode=pl.Buffered(1))


def _params(n_axes):
    return pltpu.CompilerParams(dimension_semantics=("arbitrary",) * n_axes,
                                vmem_limit_bytes=VMEM_LIMIT)


def _norm_proj_kernel(x_ref, g_ref, w_ref, o_ref):
    h = _rms(x_ref[...], g_ref[...]).astype(BF16)
    o_ref[...] = _dot(h, w_ref[...])


def _norm_proj(x, g, w):
    t, d = x.shape
    n = w.shape[1]
    return pl.pallas_call(
        _norm_proj_kernel,
        grid=(t // TOKEN_TILE,),
        in_specs=[pl.BlockSpec((TOKEN_TILE, d), lambda i: (i, 0)),
                  _resident((1, d)), _resident((d, n))],
        out_specs=pl.BlockSpec((TOKEN_TILE, n), lambda i: (i, 0)),
        out_shape=jax.ShapeDtypeStruct((t, n), F32),
        compiler_params=_params(1),
        name="norm_proj",
    )(x, g.reshape(1, d), w)


def _mixer_constants():
    c = CHUNK
    mats = [np.tril(np.ones((c, c), np.float32))]
    masks = [np.eye(c, dtype=np.float32)]
    idx = np.arange(c)
    for level in range(1, N_LEVELS):
        bs = 2 ** level
        half = bs // 2
        m = np.zeros((c, c), np.float32)
        for r in range(c):
            mid = (r // bs) * bs + half - 1
            if r % bs >= half:
                m[r, mid + 1:r + 1] = 1.0
            else:
                m[r, r + 1:mid + 1] = 1.0
        mats.append(m)
        same = (idx[:, None] // bs) == (idx[None, :] // bs)
        upper_t = (idx[:, None] % bs) >= half
        lower_s = (idx[None, :] % bs) < half
        masks.append((same & upper_t & lower_s).astype(np.float32))
    rev = np.zeros((c, c), np.float32)
    for r in range(c):
        rev[r, r + 1:] = 1.0
    mats.append(rev)
    return np.concatenate(mats, axis=0), np.stack(masks, axis=0)


def _mixer_kernel(proj_ref, lbt_ref, anorm_ref, lng_ref, lnb_ref, ws_ref, bb_ref, cm_ref, lm_ref,
                  o_ref, st_ref, r_ref):
    @pl.when(pl.program_id(1) == 0)
    def _():
        st_ref[...] = jnp.zeros_like(st_ref)

    t0, t1, t2 = lbt_ref[0:1, :], lbt_ref[1:2, :], lbt_ref[2:3, :]
    mx = jnp.maximum(jnp.maximum(t0, t1), t2)
    e0, e1, e2 = jnp.exp(t0 - mx), jnp.exp(t1 - mx), jnp.exp(t2 - mx)
    lb = e0 / (e0 + e1 + e2)

    w = A_WIDTH
    f = lb + (1.0 - lb) * _sigmoid(proj_ref[:, w:2 * w])
    logf = jnp.log(f) * LOG2E
    hi = logf.astype(BF16)
    lo = (logf - hi.astype(F32)).astype(BF16)
    r_ref[...] = _dot(cm_ref[...], jnp.concatenate([hi, lo], axis=0))

    c = CHUNK
    for h in range(A_HEADS):
        sl = slice(h * A_DK, (h + 1) * A_DK)
        q = proj_ref[:, sl]
        qf = q * _sigmoid(q)
        kk = 1.0 - f[:, sl]
        qb, kb = qf.astype(BF16), kk.astype(BF16)
        vb = proj_ref[:, 2 * w + h * A_DK:2 * w + (h + 1) * A_DK].astype(BF16)
        gate = proj_ref[:, 3 * w + h * A_DK:3 * w + (h + 1) * A_DK]

        a = lm_ref[0] * _dot_nt(qb, kb)
        for level in range(1, N_LEVELS):
            xb = jnp.exp2(r_ref[level * c:(level + 1) * c, sl]).astype(BF16)
            a = a + lm_ref[level] * _dot_nt(qb * xb, kb * xb)

        eg = jnp.exp2(r_ref[0:c, sl])
        st = st_ref[h]
        o = _dot_nt((qf * eg).astype(BF16), st.astype(BF16)) + _dot(a.astype(BF16), vb)
        khat = (kk * jnp.exp2(r_ref[N_LEVELS * c:(N_LEVELS + 1) * c, sl])).astype(BF16)
        st_ref[h] = st * eg[c - 1:c, :] + _dot_tn(vb, khat)

        o = _rms(o, anorm_ref[:, sl])
        o_ref[:, sl] = (o * (gate * _sigmoid(gate))).astype(o_ref.dtype)

    gu = _gelu_tanh(proj_ref[:, 4 * w:4 * w + B_WIDTH])
    gv = _gelu_tanh(proj_ref[:, 4 * w + B_WIDTH:4 * w + 2 * B_WIDTH])
    mu = jnp.mean(gv, axis=-1, keepdims=True)
    d = gv - mu
    var = jnp.mean(d * d, axis=-1, keepdims=True)
    vn = (d * lax.rsqrt(var + EPS) * lng_ref[...] + lnb_ref[...]).astype(BF16)
    row = lax.broadcasted_iota(jnp.int32, (c, c), 0)
    col = lax.broadcasted_iota(jnp.int32, (c, c), 1)
    gd = B_WIDTH // B_GROUPS
    for g in range(B_GROUPS):
        sl = slice(g * gd, (g + 1) * gd)
        wg = jnp.where(row >= col, ws_ref[g], 0.0).astype(BF16)
        mixed = _dot(wg, vn[:, sl]) + bb_ref[:, g:g + 1]
        o_ref[:, A_WIDTH + g * gd:A_WIDTH + (g + 1) * gd] = (gu[:, sl] * mixed).astype(o_ref.dtype)


def _mixer(proj, batch, lb_table, a_norm, ln_g, ln_b, w_s, b_bias):
    t, n = proj.shape
    seq = t // batch
    n_chunks = seq // CHUNK
    cm, lm = _mixer_constants()
    cm = np.concatenate([cm, cm], axis=1)
    width = A_WIDTH + B_WIDTH
    return pl.pallas_call(
        _mixer_kernel,
        grid=(batch, n_chunks),
        in_specs=[pl.BlockSpec((CHUNK, n), lambda b, c: (b * n_chunks + c, 0)),
                  _resident(lb_table.shape), _resident((1, A_WIDTH)),
                  _resident((1, B_WIDTH)), _resident((1, B_WIDTH)),
                  _resident(w_s.shape), _resident((CHUNK, B_GROUPS)),
                  _resident(cm.shape), _resident(lm.shape)],
        out_specs=pl.BlockSpec((CHUNK, width), lambda b, c: (b * n_chunks + c, 0)),
        out_shape=jax.ShapeDtypeStruct((t, width), BF16),
        scratch_shapes=[pltpu.VMEM((A_HEADS, A_DK, A_DK), F32),
                        pltpu.VMEM(((N_LEVELS + 1) * CHUNK, A_WIDTH), F32)],
        compiler_params=_params(2),
        name="mixer",
    )(proj, lb_table, a_norm.reshape(1, -1), ln_g.reshape(1, -1), ln_b.reshape(1, -1),
      w_s, b_bias.T, jnp.asarray(cm, BF16), jnp.asarray(lm, F32))


def _out_ffn_kernel(mix_ref, x_ref, wo_ref, w1_ref, w2_ref, gpost_ref, gpre_ref, gffn_ref, o_ref, slab_ref,
                    *, to_residue_major):
    d = wo_ref.shape[1]
    d_ff = w1_ref.shape[1]
    lanes = V7X_LANES
    rows = TOKEN_TILE // FFN_PARTS
    per = TILE_PER_RESIDUE // FFN_PARTS
    heads = []
    for part in range(FFN_PARTS):
        if to_residue_major:
            x, mix = x_ref[part * rows:(part + 1) * rows, :], mix_ref[part * rows:(part + 1) * rows, :]
        else:
            x = jnp.concatenate([x_ref[r, part * per:(part + 1) * per, :] for r in range(RESIDUES)], axis=0)
            mix = jnp.concatenate([mix_ref[r, part * per:(part + 1) * per, :] for r in range(RESIDUES)], axis=0)
        x1 = x + _rms(_dot(mix.astype(BF16), wo_ref[...]), gpost_ref[...])
        heads.append((x1, _rms(x1, gpre_ref[...]).astype(BF16)))
    for part, (x1, h) in enumerate(heads):
        y = jnp.zeros_like(x1)
        for j in range(d_ff // d):
            hid = _dot(h, w1_ref[:, j * d:(j + 1) * d])
            hid = jnp.square(jnp.maximum(hid, 0.0)).astype(BF16)
            y = y + _dot(hid, w2_ref[j * d:(j + 1) * d, :])
        x2 = x1 + _rms(y, gffn_ref[...])
        nat = slice(part * rows, (part + 1) * rows)
        for cb in range(d // lanes):
            cols = slice(cb * lanes, (cb + 1) * lanes)
            if to_residue_major:
                slab_ref[cb, nat, :] = x2[:, cols]
                for r in range(RESIDUES):
                    o_ref[r, part * per:(part + 1) * per, cols] = (
                        slab_ref[cb, pl.ds(part * rows + r, per, stride=RESIDUES), :])
            else:
                for r in range(RESIDUES):
                    slab_ref[cb, pl.ds(part * rows + r, per, stride=RESIDUES), :] = (
                        x2[r * per:(r + 1) * per, cols])
                o_ref[nat, cols] = slab_ref[cb, nat, :]


def _out_ffn(mix, x, w_out, w1, w2, g_post, g_pre, g_ffn, *, to_residue_major):
    d = w_out.shape[1]
    d_ff = w1.shape[1]
    per = TILE_PER_RESIDUE
    weights = [_resident((d, d)), _resident((d, d_ff)), _resident((d_ff, d)),
               _resident((1, d)), _resident((1, d)), _resident((1, d))]
    args = (w_out.astype(BF16), w1.astype(BF16), w2.astype(BF16),
            g_post.reshape(1, d), g_pre.reshape(1, d), g_ffn.reshape(1, d))
    scratch = [pltpu.VMEM((d // V7X_LANES, TOKEN_TILE, V7X_LANES), F32)]
    kern = functools.partial(_out_ffn_kernel, to_residue_major=to_residue_major)
    if to_residue_major:
        batch, seq, _ = x.shape
        tiles = seq // TOKEN_TILE
        nat = pl.BlockSpec((None, TOKEN_TILE, d), lambda b, j: (b, j, 0))
        return pl.pallas_call(
            kern,
            grid=(batch, tiles),
            in_specs=[nat, nat] + weights,
            out_specs=pl.BlockSpec((None, RESIDUES, per, d), lambda b, j: (b, 0, j, 0)),
            out_shape=jax.ShapeDtypeStruct((batch, RESIDUES, seq // RESIDUES, d), F32),
            scratch_shapes=scratch,
            compiler_params=_params(2),
            name="out_ffn_to_residue_major",
        )(mix, x, *args)
    batch, _, per_seq, _ = x.shape
    tiles = per_seq // per
    res = pl.BlockSpec((None, RESIDUES, per, d), lambda b, j: (b, 0, j, 0))
    return pl.pallas_call(
        kern,
        grid=(batch, tiles),
        in_specs=[res, res] + weights,
        out_specs=pl.BlockSpec((None, TOKEN_TILE, d), lambda b, j: (b, j, 0)),
        out_shape=jax.ShapeDtypeStruct((batch, RESIDUES * per_seq, d), F32),
        scratch_shapes=scratch,
        compiler_params=_params(2),
        name="out_ffn_from_residue_major",
    )(mix, x, *args)


def _qkv_rope_kernel(x_ref, g_ref, w_ref, cos_ref, sin_hi_ref, sin_lo_ref, o_ref):
    d = w_ref.shape[0]
    h = _rms(x_ref[...], g_ref[...]).astype(BF16)
    cos, sin_hi, sin_lo = cos_ref[...], sin_hi_ref[...], sin_lo_ref[...]
    lanes = V7X_LANES
    rot = C_HEAD_DIM // 8
    wide = V7X_MXU_WIDTH
    for j in range(3 * d // wide):
        blk = _dot(h, w_ref[:, j * wide:(j + 1) * wide])
        for half in range(wide // lanes):
            col = j * wide + half * lanes
            part = blk[:, half * lanes:(half + 1) * lanes]
            if col < 2 * d:
                part = (part * cos + pltpu.roll(part, rot, 1) * sin_hi
                        + pltpu.roll(part, lanes - rot, 1) * sin_lo)
            if col < d:
                part = part * (LOG2E / np.sqrt(C_HEAD_DIM))
            o_ref[:, col:col + lanes] = part


def _qkv_rope(x, g, w, cos, sin_hi, sin_lo):
    t, d = x.shape
    n = w.shape[1]
    tab = pl.BlockSpec((TOKEN_TILE, V7X_LANES), lambda i: (i, 0))
    return pl.pallas_call(
        _qkv_rope_kernel,
        grid=(t // TOKEN_TILE,),
        in_specs=[pl.BlockSpec((TOKEN_TILE, d), lambda i: (i, 0)),
                  _resident((1, d)), _resident((d, n)), tab, tab, tab],
        out_specs=pl.BlockSpec((TOKEN_TILE, n), lambda i: (i, 0)),
        out_shape=jax.ShapeDtypeStruct((t, n), F32),
        compiler_params=_params(1),
        name="qkv_rope",
    )(x, g.reshape(1, d), w.astype(BF16), cos, sin_hi, sin_lo)


def _rope_tables(positions):
    batch, seq = positions.shape
    pos = positions.reshape(batch, seq // RESIDUES, RESIDUES).transpose(0, 2, 1).reshape(-1)
    half = C_HEAD_DIM // 8
    inv = ROPE_THETA ** (-jnp.arange(half, dtype=F32) / half)
    ang = pos[:, None].astype(F32) * inv
    cos, sin = jnp.cos(ang), jnp.sin(ang)
    one = jnp.ones((ang.shape[0], C_HEAD_DIM - 2 * half), F32)
    zero = jnp.zeros_like(cos)
    cos64 = jnp.concatenate([cos, cos, one], axis=1)
    hi64 = jnp.concatenate([zero, sin, 0.0 * one], axis=1)
    lo64 = jnp.concatenate([-sin, zero, 0.0 * one], axis=1)
    reps = V7X_LANES // C_HEAD_DIM
    return jnp.tile(cos64, (1, reps)), jnp.tile(hi64, (1, reps)), jnp.tile(lo64, (1, reps))


def _attention_biases():
    n = ATT_BLOCK
    out = {}
    for dil in (1, 4):
        chunks = RESIDUES // dil
        per = n // chunks
        rho = np.arange(n)
        pos = chunks * (rho % per) + rho // per
        kpos = np.concatenate([pos - n, pos])
        dist = pos[:, None] - kpos[None, :]
        out[dil] = np.where((dist >= 0) & (dist <= n), 0.0, NEG).astype(np.float32)
    rho = np.arange(n)
    out[16] = np.where(rho[:, None] >= rho[None, :], 0.0, NEG).astype(np.float32)
    first = np.concatenate([np.full((n, n), NEG, np.float32), np.zeros((n, n), np.float32)], axis=1)
    eye2 = np.concatenate([np.eye(n, dtype=np.float32)] * 2, axis=0)
    masks = (out[16], out[4], out[4] + first, out[1], out[1] + first)
    return (eye2,) + tuple(np.ascontiguousarray(b.T) for b in masks)


def _attn_kernel(q_ref, k_ref, v_ref, eye_ref, bt16_ref, bt4_ref, bt4f_ref, bt1_ref, bt1f_ref,
                 o_ref, m_s, l_s, acc_s):
    n = ATT_BLOCK
    lanes = V7X_LANES
    head0 = lax.broadcasted_iota(jnp.int32, (n, lanes), 1) < C_HEAD_DIM

    def attend(blocks):
        eye2 = eye_ref[...]
        scores = []
        for q, k, _, mask_t in blocks:
            q2 = jnp.concatenate([jnp.where(head0, q, 0.0), jnp.where(head0, 0.0, q)], axis=0)
            scores.append(_dot_nt(jnp.concatenate([q2.astype(BF16), eye2], axis=1),
                                  jnp.concatenate([k.astype(BF16), mask_t], axis=1)))
        outs = []
        for (_, _, v, _), s in zip(blocks, scores):
            m = jnp.max(s, axis=-1, keepdims=True)
            p = jnp.exp2(s - m).astype(BF16)
            va = jnp.concatenate([v.astype(BF16), jnp.ones(v.shape, BF16)], axis=1)
            pv = _dot(p, va)
            outs.append((jnp.where(head0, m[:n], m[n:]),
                         jnp.where(head0, pv[:n, lanes:], pv[n:, lanes:]),
                         jnp.where(head0, pv[:n, :lanes], pv[n:, :lanes])))
        return outs

    def gather(ref, starts, size):
        return jnp.concatenate([ref[pl.ds(s, size), :] for s in starts], axis=0)

    def merge(old, new):
        m_o, l_o, a_o = old
        m_n, l_n, a_n = new
        m = jnp.maximum(m_o, m_n)
        w_o, w_n = jnp.exp2(m_o - m), jnp.exp2(m_n - m)
        return m, l_o * w_o + l_n * w_n, a_o * w_o + a_n * w_n

    def body16(i, carry):
        starts = [pl.multiple_of((i * ATT_UNROLL + u) * n, n) for u in range(ATT_UNROLL)]
        blocks = [(q_ref[pl.ds(s, n), :], k_ref[pl.ds(s, n), :], v_ref[pl.ds(s, n), :], bt16_ref[...])
                  for s in starts]
        for s, (m, l, acc) in zip(starts, attend(blocks)):
            m_s[pl.ds(s, n), :] = m
            l_s[pl.ds(s, n), :] = l
            acc_s[pl.ds(s, n), :] = acc
        return carry

    lax.fori_loop(0, RESIDUES // ATT_UNROLL, body16, 0)

    def strided_blocks(dil, runs, mask_ref, mask_first_ref):
        chunks = RESIDUES // dil
        per = n // chunks
        blocks, all_starts = [], []
        for res, first_blk, count in runs:

            def starts_of(blk, res=res):
                return [pl.multiple_of((res + dil * c) * n + per * blk, per) for c in range(chunks)]

            static_first = isinstance(first_blk, int)
            prev0 = max(first_blk - 1, 0) if static_first else jnp.maximum(first_blk - 1, 0)
            k_prev, v_prev = gather(k_ref, starts_of(prev0), per), gather(v_ref, starts_of(prev0), per)
            for u in range(count):
                starts = starts_of(first_blk + u)
                k_cur, v_cur = gather(k_ref, starts, per), gather(v_ref, starts, per)
                mask_t = mask_ref[...]
                if u == 0 and static_first and first_blk == 0:
                    mask_t = mask_first_ref[...]
                elif u == 0 and not static_first:
                    mask_t = jnp.where(first_blk == 0, mask_first_ref[...], mask_t)
                blocks.append((gather(q_ref, starts, per), jnp.concatenate([k_prev, k_cur], axis=0),
                               jnp.concatenate([v_prev, v_cur], axis=0), mask_t))
                all_starts.append(starts)
                k_prev, v_prev = k_cur, v_cur
        merged = []
        for starts, new in zip(all_starts, attend(blocks)):
            old = (gather(m_s, starts, per), gather(l_s, starts, per), gather(acc_s, starts, per))
            merged.append((starts, per, merge(old, new)))
        return merged

    blocks4 = RESIDUES // 4
    res_per_iter = ATT_UNROLL // blocks4

    def body4(i, carry):
        runs = [(i * res_per_iter + j, 0, blocks4) for j in range(res_per_iter)]
        for starts, per, (m, l, acc) in strided_blocks(4, runs, bt4_ref, bt4f_ref):
            for c, s in enumerate(starts):
                m_s[pl.ds(s, per), :] = m[c * per:(c + 1) * per, :]
                l_s[pl.ds(s, per), :] = l[c * per:(c + 1) * per, :]
                acc_s[pl.ds(s, per), :] = acc[c * per:(c + 1) * per, :]
        return carry

    lax.fori_loop(0, 4 // res_per_iter, body4, 0)

    def body1(i, carry):
        for starts, per, (m, l, acc) in strided_blocks(1, [(0, i * ATT_UNROLL, ATT_UNROLL)], bt1_ref, bt1f_ref):
            out = acc / l
            for c, s in enumerate(starts):
                o_ref[pl.ds(s, per), :] = out[c * per:(c + 1) * per, :]
        return carry

    lax.fori_loop(0, RESIDUES // ATT_UNROLL, body1, 0)


def _attention(qkv, batch):
    t, n3 = qkv.shape
    seq = t // batch
    d = n3 // 3
    pairs = d // V7X_LANES
    qkv3 = qkv.reshape(batch, seq, n3)
    consts = [jnp.asarray(c, BF16) for c in _attention_biases()]

    def col(off):
        return pl.BlockSpec((None, seq, V7X_LANES), lambda b, p: (b, 0, off + p))

    return pl.pallas_call(
        _attn_kernel,
        grid=(batch, pairs),
        in_specs=[col(0), col(pairs), col(2 * pairs)] + [_resident(c.shape) for c in consts],
        out_specs=pl.BlockSpec((None, seq, V7X_LANES), lambda b, p: (b, 0, p)),
        out_shape=jax.ShapeDtypeStruct((batch, seq, d), F32),
        scratch_shapes=[pltpu.VMEM((seq, V7X_LANES), F32)] * 3,
        compiler_params=_params(2),
        name="dilated_attention",
    )(qkv3, qkv3, qkv3, *consts)


def kernel(x, positions, norm_mix_pre, norm_mix_post, norm_ffn_pre, norm_ffn_post, w_in_even, lb_table,
           a_norm, b_ln_g, b_ln_b, b_ws, b_bias, w_out_even, w_in_odd, w_out_odd, w_ff1, w_ff2):
    batch, seq, d = x.shape
    assert seq == RESIDUES * ATT_BLOCK and d == A_WIDTH + B_WIDTH
    assert norm_mix_pre.shape[0] == 2 and lb_table.shape[0] == 3
    t = batch * seq

    proj = _norm_proj(x.reshape(t, d), norm_mix_pre[0], w_in_even[0].astype(BF16))
    mix = _mixer(proj, batch, lb_table, a_norm[0], b_ln_g[0], b_ln_b[0], b_ws[0], b_bias[0])
    x1 = _out_ffn(mix.reshape(batch, seq, d), x, w_out_even[0], w_ff1[0], w_ff2[0],
                  norm_mix_post[0], norm_ffn_pre[0], norm_ffn_post[0], to_residue_major=True)

    cos, sin_hi, sin_lo = _rope_tables(positions)
    qkv = _qkv_rope(x1.reshape(t, d), norm_mix_pre[1], w_in_odd[0], cos, sin_hi, sin_lo)
    att = _attention(qkv, batch).reshape(x1.shape)
    return _out_ffn(att, x1, w_out_odd[0], w_ff1[1], w_ff2[1],
                    norm_mix_post[1], norm_ffn_pre[1], norm_ffn_post[1], to_residue_major=False)
```

```python
import functools

import numpy as np
import jax
import jax.numpy as jnp
from jax import lax
from jax.experimental import pallas as pl
from jax.experimental.pallas import tpu as pltpu

F32 = jnp.float32
BF16 = jnp.bfloat16

EPS = 1e-6
ROPE_THETA = 500000.0
NEG = -1e30
LOG2E = float(np.log2(np.e))

V7X_LANES = 128
V7X_MXU_WIDTH = 256
V7X_VMEM_BYTES = 64 * 1024 * 1024
VMEM_LIMIT = V7X_VMEM_BYTES - 8 * 1024 * 1024

A_WIDTH = 512
A_HEADS = 4
A_DK = 128
B_WIDTH = 512
B_GROUPS = 4
CHUNK = 128
MIXER_CHUNKS = 2
N_LEVELS = 8
C_HEAD_DIM = 64
ROPE_COLS = 32
ATT_BLOCK = 128
ATT_UNROLL = 8
RESIDUES = 16
TOKEN_TILE = 512
TILE_PER_RESIDUE = TOKEN_TILE // RESIDUES
FFN_PARTS = 2


def _dot(a, b):
    return jnp.dot(a, b, preferred_element_type=F32)


def _dot_nt(a, b):
    return lax.dot_general(a, b, (((1,), (1,)), ((), ())), preferred_element_type=F32)


def _dot_tn(a, b):
    return lax.dot_general(a, b, (((0,), (0,)), ((), ())), preferred_element_type=F32)


def _rms(x, g):
    ms = jnp.mean(x * x, axis=-1, keepdims=True)
    return x * lax.rsqrt(ms + EPS) * g


def _sigmoid(x):
    return 1.0 / (1.0 + jnp.exp(-x))


def _gelu_tanh(x):
    c = np.float32(np.sqrt(2.0 / np.pi))
    return x * (0.5 * (1.0 + jnp.tanh(c * (x + 0.044715 * (x * x * x)))))


def _resident(shape):
    nd = len(shape)
    return pl.BlockSpec(shape, lambda *_: (0,) * nd, pipeline_mode=pl.Buffered(1))


def _params(n_axes):
    return pltpu.CompilerParams(dimension_semantics=("arbitrary",) * n_axes,
                                vmem_limit_bytes=VMEM_LIMIT)


def _norm_proj_kernel(x_ref, g_ref, w_ref, o_ref):
    h = _rms(x_ref[...], g_ref[...]).astype(BF16)
    o_ref[...] = _dot(h, w_ref[...])


def _norm_proj(x, g, w):
    t, d = x.shape
    n = w.shape[1]
    return pl.pallas_call(
        _norm_proj_kernel,
        grid=(t // TOKEN_TILE,),
        in_specs=[pl.BlockSpec((TOKEN_TILE, d), lambda i: (i, 0)),
                  _resident((1, d)), _resident((d, n))],
        out_specs=pl.BlockSpec((TOKEN_TILE, n), lambda i: (i, 0)),
        out_shape=jax.ShapeDtypeStruct((t, n), F32),
        compiler_params=_params(1),
        name="norm_proj",
    )(x, g.reshape(1, d), w)


def _mixer_constants():
    c = CHUNK
    mats = [np.tril(np.ones((c, c), np.float32))]
    masks = [np.eye(c, dtype=np.float32)]
    idx = np.arange(c)
    for level in range(1, N_LEVELS):
        bs = 2 ** level
        half = bs // 2
        m = np.zeros((c, c), np.float32)
        for r in range(c):
            mid = (r // bs) * bs + half - 1
            if r % bs >= half:
                m[r, mid + 1:r + 1] = 1.0
            else:
                m[r, r + 1:mid + 1] = 1.0
        mats.append(m)
        same = (idx[:, None] // bs) == (idx[None, :] // bs)
        upper_t = (idx[:, None] % bs) >= half
        lower_s = (idx[None, :] % bs) < half
        masks.append((same & upper_t & lower_s).astype(np.float32))
    rev = np.zeros((c, c), np.float32)
    for r in range(c):
        rev[r, r + 1:] = 1.0
    mats.append(rev)
    return np.concatenate(mats, axis=0), np.stack(masks, axis=0)


def _mixer_kernel(proj_ref, lbt_ref, anorm_ref, lng_ref, lnb_ref, ws_ref, bb_ref, cm_ref, lm_ref,
                  o_ref, st_ref, r_ref):
    @pl.when(pl.program_id(1) == 0)
    def _():
        st_ref[...] = jnp.zeros_like(st_ref)

    t0, t1, t2 = lbt_ref[0:1, :], lbt_ref[1:2, :], lbt_ref[2:3, :]
    mx = jnp.maximum(jnp.maximum(t0, t1), t2)
    e0, e1, e2 = jnp.exp(t0 - mx), jnp.exp(t1 - mx), jnp.exp(t2 - mx)
    lb = e0 / (e0 + e1 + e2)

    w = A_WIDTH
    c = CHUNK
    chunk_rows = [slice(s * c, (s + 1) * c) for s in range(MIXER_CHUNKS)]
    f = lb + (1.0 - lb) * _sigmoid(proj_ref[:, w:2 * w])
    logf = jnp.log(f) * LOG2E
    hi = logf.astype(BF16)
    lo = (logf - hi.astype(F32)).astype(BF16)
    r_ref[...] = _dot(cm_ref[...], jnp.concatenate(
        [jnp.concatenate([hi[rows], lo[rows]], axis=0) for rows in chunk_rows], axis=1))

    chains = [(h, s) for h in range(A_HEADS) for s in range(MIXER_CHUNKS)]

    def cols_of(h, s):
        return slice(h * A_DK, (h + 1) * A_DK), slice(s * w + h * A_DK, s * w + (h + 1) * A_DK)

    ops = {}
    for h, s in chains:
        sl, _ = cols_of(h, s)
        rows = chunk_rows[s]
        q = proj_ref[rows, sl]
        qf = q * _sigmoid(q)
        kk = 1.0 - f[rows, sl]
        ops[h, s] = (qf, kk, qf.astype(BF16), kk.astype(BF16))
    scores = {key: lm_ref[0] * _dot_nt(qb, kb) for key, (_, _, qb, kb) in ops.items()}
    for level in range(1, N_LEVELS):
        for h, s in chains:
            _, rs = cols_of(h, s)
            _, _, qb, kb = ops[h, s]
            xb = jnp.exp2(r_ref[level * c:(level + 1) * c, rs]).astype(BF16)
            scores[h, s] = scores[h, s] + lm_ref[level] * _dot_nt(qb * xb, kb * xb)

    for h in range(A_HEADS):
        st = st_ref[h]
        for s, rows in enumerate(chunk_rows):
            sl, rs = cols_of(h, s)
            qf, kk, _, _ = ops[h, s]
            vb = proj_ref[rows, 2 * w + h * A_DK:2 * w + (h + 1) * A_DK].astype(BF16)
            eg = jnp.exp2(r_ref[0:c, rs])
            o = _dot_nt((qf * eg).astype(BF16), st.astype(BF16)) + _dot(scores[h, s].astype(BF16), vb)
            khat = (kk * jnp.exp2(r_ref[N_LEVELS * c:(N_LEVELS + 1) * c, rs])).astype(BF16)
            st = st * eg[c - 1:c, :] + _dot_tn(vb, khat)
            gate = proj_ref[rows, 3 * w + h * A_DK:3 * w + (h + 1) * A_DK]
            o_ref[rows, sl] = (_rms(o, anorm_ref[:, sl]) * (gate * _sigmoid(gate))).astype(o_ref.dtype)
        st_ref[h] = st

    gu = _gelu_tanh(proj_ref[:, 4 * w:4 * w + B_WIDTH])
    gv = _gelu_tanh(proj_ref[:, 4 * w + B_WIDTH:4 * w + 2 * B_WIDTH])
    mu = jnp.mean(gv, axis=-1, keepdims=True)
    d = gv - mu
    var = jnp.mean(d * d, axis=-1, keepdims=True)
    vn = (d * lax.rsqrt(var + EPS) * lng_ref[...] + lnb_ref[...]).astype(BF16)
    row = lax.broadcasted_iota(jnp.int32, (c, c), 0)
    col = lax.broadcasted_iota(jnp.int32, (c, c), 1)
    gd = B_WIDTH // B_GROUPS
    for g in range(B_GROUPS):
        sl = slice(g * gd, (g + 1) * gd)
        wg = jnp.where(row >= col, ws_ref[g], 0.0).astype(BF16)
        for rows in chunk_rows:
            mixed = _dot(wg, vn[rows, sl]) + bb_ref[:, g:g + 1]
            o_ref[rows, A_WIDTH + g * gd:A_WIDTH + (g + 1) * gd] = (gu[rows, sl] * mixed).astype(o_ref.dtype)


def _mixer(proj, batch, lb_table, a_norm, ln_g, ln_b, w_s, b_bias):
    t, n = proj.shape
    seq = t // batch
    rows = CHUNK * MIXER_CHUNKS
    n_steps = seq // rows
    cm, lm = _mixer_constants()
    cm = np.concatenate([cm, cm], axis=1)
    width = A_WIDTH + B_WIDTH
    return pl.pallas_call(
        _mixer_kernel,
        grid=(batch, n_steps),
        in_specs=[pl.BlockSpec((rows, n), lambda b, c: (b * n_steps + c, 0)),
                  _resident(lb_table.shape), _resident((1, A_WIDTH)),
                  _resident((1, B_WIDTH)), _resident((1, B_WIDTH)),
                  _resident(w_s.shape), _resident((CHUNK, B_GROUPS)),
                  _resident(cm.shape), _resident(lm.shape)],
        out_specs=pl.BlockSpec((rows, width), lambda b, c: (b * n_steps + c, 0)),
        out_shape=jax.ShapeDtypeStruct((t, width), BF16),
        scratch_shapes=[pltpu.VMEM((A_HEADS, A_DK, A_DK), F32),
                        pltpu.VMEM(((N_LEVELS + 1) * CHUNK, MIXER_CHUNKS * A_WIDTH), F32)],
        compiler_params=_params(2),
        name="mixer",
    )(proj, lb_table, a_norm.reshape(1, -1), ln_g.reshape(1, -1), ln_b.reshape(1, -1),
      w_s, b_bias.T, jnp.asarray(cm, BF16), jnp.asarray(lm, F32))


def _out_ffn_kernel(mix_ref, x_ref, wo_ref, w1_ref, w2_ref, gpost_ref, gpre_ref, gffn_ref, o_ref, slab_ref,
                    *, to_residue_major):
    d = wo_ref.shape[1]
    d_ff = w1_ref.shape[1]
    lanes = V7X_LANES
    rows = TOKEN_TILE // FFN_PARTS
    per = TILE_PER_RESIDUE // FFN_PARTS
    heads = []
    for part in range(FFN_PARTS):
        if to_residue_major:
            x, mix = x_ref[part * rows:(part + 1) * rows, :], mix_ref[part * rows:(part + 1) * rows, :]
        else:
            x = jnp.concatenate([x_ref[r, part * per:(part + 1) * per, :] for r in range(RESIDUES)], axis=0)
            mix = jnp.concatenate([mix_ref[r, part * per:(part + 1) * per, :] for r in range(RESIDUES)], axis=0)
        x1 = x + _rms(_dot(mix.astype(BF16), wo_ref[...]), gpost_ref[...])
        heads.append((x1, _rms(x1, gpre_ref[...]).astype(BF16)))
    for part, (x1, h) in enumerate(heads):
        y = jnp.zeros_like(x1)
        for j in range(d_ff // d):
            hid = _dot(h, w1_ref[:, j * d:(j + 1) * d])
            hid = jnp.square(jnp.maximum(hid, 0.0)).astype(BF16)
            y = y + _dot(hid, w2_ref[j * d:(j + 1) * d, :])
        x2 = x1 + _rms(y, gffn_ref[...])
        nat = slice(part * rows, (part + 1) * rows)
        for cb in range(d // lanes):
            cols = slice(cb * lanes, (cb + 1) * lanes)
            if to_residue_major:
                slab_ref[cb, nat, :] = x2[:, cols]
                for r in range(RESIDUES):
                    o_ref[r, part * per:(part + 1) * per, cols] = (
                        slab_ref[cb, pl.ds(part * rows + r, per, stride=RESIDUES), :])
            else:
                for r in range(RESIDUES):
                    slab_ref[cb, pl.ds(part * rows + r, per, stride=RESIDUES), :] = (
                        x2[r * per:(r + 1) * per, cols])
                o_ref[nat, cols] = slab_ref[cb, nat, :]


def _out_ffn(mix, x, w_out, w1, w2, g_post, g_pre, g_ffn, *, to_residue_major):
    d = w_out.shape[1]
    d_ff = w1.shape[1]
    per = TILE_PER_RESIDUE
    weights = [_resident((d, d)), _resident((d, d_ff)), _resident((d_ff, d)),
               _resident((1, d)), _resident((1, d)), _resident((1, d))]
    args = (w_out.astype(BF16), w1.astype(BF16), w2.astype(BF16),
            g_post.reshape(1, d), g_pre.reshape(1, d), g_ffn.reshape(1, d))
    scratch = [pltpu.VMEM((d // V7X_LANES, TOKEN_TILE, V7X_LANES), F32)]
    kern = functools.partial(_out_ffn_kernel, to_residue_major=to_residue_major)
    if to_residue_major:
        batch, seq, _ = x.shape
        tiles = seq // TOKEN_TILE
        nat = pl.BlockSpec((None, TOKEN_TILE, d), lambda b, j: (b, j, 0))
        return pl.pallas_call(
            kern,
            grid=(batch, tiles),
            in_specs=[nat, nat] + weights,
            out_specs=pl.BlockSpec((None, RESIDUES, per, d), lambda b, j: (b, 0, j, 0)),
            out_shape=jax.ShapeDtypeStruct((batch, RESIDUES, seq // RESIDUES, d), F32),
            scratch_shapes=scratch,
            compiler_params=_params(2),
            name="out_ffn_to_residue_major",
        )(mix, x, *args)
    batch, _, per_seq, _ = x.shape
    tiles = per_seq // per
    res = pl.BlockSpec((None, RESIDUES, per, d), lambda b, j: (b, 0, j, 0))
    return pl.pallas_call(
        kern,
        grid=(batch, tiles),
        in_specs=[res, res] + weights,
        out_specs=pl.BlockSpec((None, TOKEN_TILE, d), lambda b, j: (b, j, 0)),
        out_shape=jax.ShapeDtypeStruct((batch, RESIDUES * per_seq, d), F32),
        scratch_shapes=scratch,
        compiler_params=_params(2),
        name="out_ffn_from_residue_major",
    )(mix, x, *args)


def _rope_expansion():
    half = C_HEAD_DIM // 8
    lanes = V7X_LANES
    e = np.zeros((ROPE_COLS, 3 * lanes), np.float32)
    for lane in range(lanes):
        c = lane % C_HEAD_DIM
        if c < 2 * half:
            e[c % half, lane] = 1.0
        else:
            e[2 * half, lane] = 1.0
        if half <= c < 2 * half:
            e[half + c - half, lanes + lane] = 1.0
        if c < half:
            e[half + c, 2 * lanes + lane] = -1.0
    return e


def _qkv_rope_kernel(x_ref, g_ref, w_ref, cs_ref, e_ref, o_ref):
    d = w_ref.shape[0]
    h = _rms(x_ref[...], g_ref[...]).astype(BF16)
    lanes = V7X_LANES
    cs = cs_ref[...]
    cs_hi = cs.astype(BF16)
    cs_lo = (cs - cs_hi.astype(F32)).astype(BF16)
    tables = _dot(cs_hi, e_ref[...]) + _dot(cs_lo, e_ref[...])
    cos, sin_hi, sin_lo = tables[:, :lanes], tables[:, lanes:2 * lanes], tables[:, 2 * lanes:]
    rot = C_HEAD_DIM // 8
    wide = V7X_MXU_WIDTH
    for j in range(3 * d // wide):
        blk = _dot(h, w_ref[:, j * wide:(j + 1) * wide])
        for half in range(wide // lanes):
            col = j * wide + half * lanes
            part = blk[:, half * lanes:(half + 1) * lanes]
            if col < 2 * d:
                part = (part * cos + pltpu.roll(part, rot, 1) * sin_hi
                        + pltpu.roll(part, lanes - rot, 1) * sin_lo)
            if col < d:
                part = part * (LOG2E / np.sqrt(C_HEAD_DIM))
            o_ref[:, col:col + lanes] = part


def _qkv_rope(x, g, w, cos_sin):
    t, d = x.shape
    n = w.shape[1]
    expand = jnp.asarray(_rope_expansion(), BF16)
    return pl.pallas_call(
        _qkv_rope_kernel,
        grid=(t // TOKEN_TILE,),
        in_specs=[pl.BlockSpec((TOKEN_TILE, d), lambda i: (i, 0)),
                  _resident((1, d)), _resident((d, n)),
                  pl.BlockSpec((TOKEN_TILE, ROPE_COLS), lambda i: (i, 0)), _resident(expand.shape)],
        out_specs=pl.BlockSpec((TOKEN_TILE, n), lambda i: (i, 0)),
        out_shape=jax.ShapeDtypeStruct((t, n), F32),
        compiler_params=_params(1),
        name="qkv_rope",
    )(x, g.reshape(1, d), w.astype(BF16), cos_sin, expand)


def _rope_cos_sin(positions):
    batch, seq = positions.shape
    pos = positions.reshape(batch, seq // RESIDUES, RESIDUES).transpose(0, 2, 1).reshape(-1)
    half = C_HEAD_DIM // 8
    inv = ROPE_THETA ** (-jnp.arange(half, dtype=F32) / half)
    ang = pos[:, None].astype(F32) * inv
    pad = jnp.ones_like(ang)
    return jnp.concatenate([jnp.cos(ang), jnp.sin(ang), pad, 0.0 * pad], axis=1)


def _attention_biases():
    n = ATT_BLOCK
    out = {}
    for dil in (1, 4):
        chunks = RESIDUES // dil
        per = n // chunks
        rho = np.arange(n)
        pos = chunks * (rho % per) + rho // per
        kpos = np.concatenate([pos - n, pos])
        dist = pos[:, None] - kpos[None, :]
        out[dil] = np.where((dist >= 0) & (dist <= n), 0.0, NEG).astype(np.float32)
    rho = np.arange(n)
    out[16] = np.where(rho[:, None] >= rho[None, :], 0.0, NEG).astype(np.float32)
    first = np.concatenate([np.full((n, n), NEG, np.float32), np.zeros((n, n), np.float32)], axis=1)
    eye2 = np.concatenate([np.eye(n, dtype=np.float32)] * 2, axis=0)
    masks = (out[16], out[4], out[4] + first, out[1], out[1] + first)
    return (eye2,) + tuple(np.ascontiguousarray(b.T) for b in masks)


def _attn_kernel(q_ref, k_ref, v_ref, eye_ref, bt16_ref, bt4_ref, bt4f_ref, bt1_ref, bt1f_ref,
                 o_ref, m_s, l_s, acc_s):
    n = ATT_BLOCK
    lanes = V7X_LANES
    head0 = lax.broadcasted_iota(jnp.int32, (n, lanes), 1) < C_HEAD_DIM

    def attend(n_blocks, load, sink):
        eye2 = eye_ref[...]
        values, scores, probs = {}, {}, {}

        def stage_scores(i):
            q, k, v, mask_t = load(i)
            q2 = jnp.concatenate([jnp.where(head0, q, 0.0), jnp.where(head0, 0.0, q)], axis=0)
            scores[i] = _dot_nt(jnp.concatenate([q2.astype(BF16), eye2], axis=1),
                                jnp.concatenate([k.astype(BF16), mask_t], axis=1))
            values[i] = jnp.concatenate([v.astype(BF16), jnp.ones(v.shape, BF16)], axis=1)

        def stage_softmax(i):
            s = scores.pop(i)
            m = jnp.max(s, axis=-1, keepdims=True)
            probs[i] = (m, jnp.exp2(s - m).astype(BF16))

        def stage_values(i):
            m, p = probs.pop(i)
            pv = _dot(p, values.pop(i))
            sink(i, (jnp.where(head0, m[:n], m[n:]),
                     jnp.where(head0, pv[:n, lanes:], pv[n:, lanes:]),
                     jnp.where(head0, pv[:n, :lanes], pv[n:, :lanes])))

        for i in range(n_blocks):
            stage_scores(i)
        for i in range(n_blocks):
            stage_softmax(i)
            stage_values(i)

    def gather(ref, starts, size):
        return jnp.concatenate([ref[pl.ds(s, size), :] for s in starts], axis=0)

    def scatter(ref, starts, size, value):
        for c, s in enumerate(starts):
            ref[pl.ds(s, size), :] = value[c * size:(c + 1) * size, :]

    def merge(old, new):
        m_o, l_o, a_o = old
        m_n, l_n, a_n = new
        m = jnp.maximum(m_o, m_n)
        w_o, w_n = jnp.exp2(m_o - m), jnp.exp2(m_n - m)
        return m, l_o * w_o + l_n * w_n, a_o * w_o + a_n * w_n

    def body16(i, carry):
        starts = [(i * ATT_UNROLL + u) * n for u in range(ATT_UNROLL)]

        def load(u):
            rows = pl.ds(starts[u], n)
            return q_ref[rows, :], k_ref[rows, :], v_ref[rows, :], bt16_ref[...]

        def sink(u, result):
            for ref, value in zip((m_s, l_s, acc_s), result):
                ref[pl.ds(starts[u], n), :] = value

        attend(ATT_UNROLL, load, sink)
        return carry

    for it in range(RESIDUES // ATT_UNROLL):
        body16(it, 0)

    def strided_blocks(dil, runs, mask_ref, mask_first_ref, finish):
        chunks = RESIDUES // dil
        per = n // chunks
        index = [(res, first_blk, u) for res, first_blk, count in runs for u in range(count)]

        def starts_of(res, blk):
            return [(res + dil * c) * n + per * blk for c in range(chunks)]

        def load(i):
            res, first_blk, u = index[i]
            static_first = isinstance(first_blk, int)
            blk = first_blk + u
            if u > 0:
                prev = blk - 1
            else:
                prev = max(blk - 1, 0) if static_first else jnp.maximum(blk - 1, 0)
            starts, pstarts = starts_of(res, blk), starts_of(res, prev)
            mask_t = mask_ref[...]
            if u == 0 and static_first and first_blk == 0:
                mask_t = mask_first_ref[...]
            elif u == 0 and not static_first:
                mask_t = jnp.where(first_blk == 0, mask_first_ref[...], mask_t)
            keys = jnp.concatenate([gather(k_ref, pstarts, per), gather(k_ref, starts, per)], axis=0)
            vals = jnp.concatenate([gather(v_ref, pstarts, per), gather(v_ref, starts, per)], axis=0)
            return gather(q_ref, starts, per), keys, vals, mask_t

        def sink(i, new):
            res, first_blk, u = index[i]
            starts = starts_of(res, first_blk + u)
            old = (gather(m_s, starts, per), gather(l_s, starts, per), gather(acc_s, starts, per))
            finish(starts, per, merge(old, new))

        attend(len(index), load, sink)

    blocks4 = RESIDUES // 4
    res_per_iter = ATT_UNROLL // blocks4

    def store_state(starts, per, state):
        for ref, value in zip((m_s, l_s, acc_s), state):
            scatter(ref, starts, per, value)

    def body4(i, carry):
        runs = [(i * res_per_iter + j, 0, blocks4) for j in range(res_per_iter)]
        strided_blocks(4, runs, bt4_ref, bt4f_ref, store_state)
        return carry

    for it in range(4 // res_per_iter):
        body4(it, 0)

    def store_result(starts, per, state):
        _, l, acc = state
        scatter(o_ref, starts, per, acc / l)

    def body1(i, carry):
        strided_blocks(1, [(0, i * ATT_UNROLL, ATT_UNROLL)], bt1_ref, bt1f_ref, store_result)
        return carry

    for it in range(RESIDUES // ATT_UNROLL):
        body1(it, 0)


def _attention(qkv, batch):
    t, n3 = qkv.shape
    seq = t // batch
    d = n3 // 3
    pairs = d // V7X_LANES
    qkv3 = qkv.reshape(batch, seq, n3)
    consts = [jnp.asarray(c, BF16) for c in _attention_biases()]

    def col(off):
        return pl.BlockSpec((None, seq, V7X_LANES), lambda b, p: (b, 0, off + p))

    return pl.pallas_call(
        _attn_kernel,
        grid=(batch, pairs),
        in_specs=[col(0), col(pairs), col(2 * pairs)] + [_resident(c.shape) for c in consts],
        out_specs=pl.BlockSpec((None, seq, V7X_LANES), lambda b, p: (b, 0, p)),
        out_shape=jax.ShapeDtypeStruct((batch, seq, d), F32),
        scratch_shapes=[pltpu.VMEM((seq, V7X_LANES), F32)] * 3,
        compiler_params=_params(2),
        name="dilated_attention",
    )(qkv3, qkv3, qkv3, *consts)


def kernel(x, positions, norm_mix_pre, norm_mix_post, norm_ffn_pre, norm_ffn_post, w_in_even, lb_table,
           a_norm, b_ln_g, b_ln_b, b_ws, b_bias, w_out_even, w_in_odd, w_out_odd, w_ff1, w_ff2):
    batch, seq, d = x.shape
    assert seq == RESIDUES * ATT_BLOCK and d == A_WIDTH + B_WIDTH
    assert norm_mix_pre.shape[0] == 2 and lb_table.shape[0] == 3
    t = batch * seq

    proj = _norm_proj(x.reshape(t, d), norm_mix_pre[0], w_in_even[0].astype(BF16))
    mix = _mixer(proj, batch, lb_table, a_norm[0], b_ln_g[0], b_ln_b[0], b_ws[0], b_bias[0])
    x1 = _out_ffn(mix.reshape(batch, seq, d), x, w_out_even[0], w_ff1[0], w_ff2[0],
                  norm_mix_post[0], norm_ffn_pre[0], norm_ffn_post[0], to_residue_major=True)

    qkv = _qkv_rope(x1.reshape(t, d), norm_mix_pre[1], w_in_odd[0], _rope_cos_sin(positions))
    att = _attention(qkv, batch).reshape(x1.shape)
    return _out_ffn(att, x1, w_out_odd[0], w_ff1[1], w_ff2[1],
                    norm_mix_post[1], norm_ffn_pre[1], norm_ffn_post[1], to_residue_major=False)
```
